```python
import jax, jax.numpy as jnp
from jax import lax
import numpy as np

D_MODEL = 1024
BATCH = 8
SEQ = 2048
DEPTH = 4
DEC_BATCH = 128
DEC_SEQ = 4
PAST_LEN = 2048
PAGE_SIZE = 128

CONV_DIM = D_MODEL
CONV_WIDTH = 31
N_HEADS = 16
N_KV = 4
HEAD_DIM = 64
Q_PER_KV = N_HEADS // N_KV
CMP_BLOCK = 32
CMP_STRIDE = 16
SEL_BLOCK = 64
SEL_TOPN = 8
WINDOW = 512
Q_BLOCK = 128
N_KV_SLOTS = 4
FORCE_BONUS = 1.0e4
GM_DIM = D_MODEL
GM_GROUPS = 4
GM_CHUNK = 128
N_GROUPS = 4
EXPERTS_PER_GROUP = 4
N_EXPERTS = N_GROUPS * EXPERTS_PER_GROUP
EXPERT_TOPK = 2
EXPERT_HIDDEN = 512
DN_ALPHA = (2 * DEPTH) ** 0.25
DN_BETA = (8 * DEPTH) ** -0.25
LN_EPS = 1e-5
NEG = -1e30
KV_W = N_KV * HEAD_DIM
IN_SIZES = (2 * CONV_DIM, N_HEADS * HEAD_DIM, 6 * KV_W, 3 * N_HEADS, 2 * GM_DIM, 3 * D_MODEL)
IN_DIM = sum(IN_SIZES)

kernel_name = "hybrid_conv_nsa_gmlp_hmoe_step"


def layer_norm(x, g, b):
    xf = x.astype(jnp.float32)
    mu = jnp.mean(xf, axis=-1, keepdims=True)
    var = jnp.mean(jnp.square(xf - mu), axis=-1, keepdims=True)
    return ((xf - mu) * lax.rsqrt(var + LN_EPS) * g.astype(jnp.float32) + b.astype(jnp.float32)).astype(x.dtype)


def masked_softmax(s, mask):
    p = jax.nn.softmax(jnp.where(mask, s.astype(jnp.float32), NEG), axis=-1)
    return jnp.where(mask, p, 0.0)


def mixer_inputs(x, w_in, b_in):
    B, T, _ = x.shape
    z = x @ w_in + b_in
    cuts = [int(c) for c in np.cumsum(IN_SIZES)[:-1]]
    conv_in, q, kv, g_nsa, gm, g_merge = jnp.split(z, cuts, axis=-1)
    a, a_gate = jnp.split(conv_in, 2, axis=-1)
    glu = a * jax.nn.sigmoid(a_gate)
    q = q.reshape(B, T, N_HEADS, HEAD_DIM)
    kv = kv.reshape(B, T, 6, N_KV, HEAD_DIM)
    g_nsa = jax.nn.sigmoid(g_nsa).reshape(B, T, N_HEADS, 3)
    u, v = jnp.split(jax.nn.gelu(gm), 2, axis=-1)
    g_merge = jax.nn.sigmoid(g_merge).reshape(B, T, 3, D_MODEL)
    return glu, q, kv, g_nsa, u, v, g_merge


def conv_module(glu_ctx, lp):
    y = lax.conv_general_dilated(glu_ctx, lp['conv_w'][:, None, :], window_strides=(1,), padding='VALID',
                                 dimension_numbers=('NWC', 'WIO', 'NWC'), feature_group_count=CONV_DIM)
    y = jax.nn.silu(layer_norm(y + lp['conv_b'], lp['conv_ln_g'], lp['conv_ln_b']))
    return y @ lp['w_up_conv']


def compress(raw, pe, w):
    B, L = raw.shape[:2]
    nc = (L - CMP_BLOCK) // CMP_STRIDE + 1
    sub = CMP_BLOCK // CMP_STRIDE
    n_chunks = nc + sub - 1
    chunks = raw[:, :n_chunks * CMP_STRIDE].reshape(B, n_chunks, CMP_STRIDE, N_KV, HEAD_DIM)
    blocks = jnp.concatenate([chunks[:, i:i + nc] for i in range(sub)], axis=2)
    return jnp.einsum('bnlgd,lde->bnge', blocks + pe[:, None, :], w)


def sel_blocks(raw):
    B, L = raw.shape[:2]
    ns = -(-L // SEL_BLOCK)
    raw = jnp.pad(raw, ((0, 0), (0, ns * SEL_BLOCK - L), (0, 0), (0, 0)))
    return raw.reshape(B, ns, SEL_BLOCK, N_KV, HEAD_DIM).transpose(0, 3, 1, 2, 4)


def nsa_keys(kv4, lp):
    kc = compress(kv4[:, :, 0], lp['cmp_pe'][0], lp['cmp_w'][0])
    vc = compress(kv4[:, :, 1], lp['cmp_pe'][1], lp['cmp_w'][1])
    return kc, vc, sel_blocks(kv4[:, :, 2]), sel_blocks(kv4[:, :, 3])


def cmp_to_sel(nc, ns):
    cs = jnp.arange(nc) * CMP_STRIDE
    ss = jnp.arange(ns) * SEL_BLOCK
    return ((cs[:, None] < ss[None, :] + SEL_BLOCK) & (cs[:, None] + CMP_BLOCK > ss[None, :])).astype(jnp.float32)


def nsa_block(q, pos, gates, kc, vc, ks, vs, kw, vw, pos_w):
    B, Tq = q.shape[:2]
    qg = q.reshape(B, Tq, N_KV, Q_PER_KV, HEAD_DIM) * (HEAD_DIM ** -0.5)
    nc = kc.shape[1]
    c_end = jnp.arange(nc) * CMP_STRIDE + CMP_BLOCK - 1
    p_c = masked_softmax(jnp.einsum('bqgrd,bngd->bgrqn', qg, kc), c_end[None, :] <= pos[:, None])
    o_c = jnp.einsum('bgrqn,bngd->bqgrd', p_c.astype(vc.dtype), vc)
    ns = ks.shape[2]
    imp = jnp.einsum('bgrqn,ns->bgqs', p_c, cmp_to_sel(nc, ns))
    blk = jnp.arange(ns)[None, :]
    cur = (pos // SEL_BLOCK)[:, None]
    valid = blk <= cur
    forced = (blk == 0) | (blk == cur) | (blk == cur - 1)
    score = jnp.where(valid, imp + jnp.where(forced, FORCE_BONUS, 0.0), -FORCE_BONUS)
    n_sel = min(SEL_TOPN, ns)
    _, idx = lax.top_k(score, n_sel)
    b_ix = jnp.arange(B)[:, None, None, None]
    g_ix = jnp.arange(N_KV)[None, :, None, None]
    kb = ks[b_ix, g_ix, idx]
    vb = vs[b_ix, g_ix, idx]
    tok = idx[..., None] * SEL_BLOCK + jnp.arange(SEL_BLOCK)
    m_s = (tok <= pos[None, None, :, None, None]).reshape(B, N_KV, 1, Tq, n_sel * SEL_BLOCK)
    s_s = jnp.einsum('bqgrd,bgqnkd->bgrqnk', qg, kb).reshape(B, N_KV, Q_PER_KV, Tq, n_sel * SEL_BLOCK)
    p_s = masked_softmax(s_s, m_s)
    o_s = jnp.einsum('bgrqm,bgqmd->bqgrd', p_s.astype(vb.dtype), vb.reshape(B, N_KV, Tq, n_sel * SEL_BLOCK, HEAD_DIM))
    m_w = (pos_w[None, :] <= pos[:, None]) & (pos_w[None, :] > pos[:, None] - WINDOW) & (pos_w[None, :] >= 0)
    p_w = masked_softmax(jnp.einsum('bqgrd,bkgd->bgrqk', qg, kw), m_w)
    o_w = jnp.einsum('bgrqk,bkgd->bqgrd', p_w.astype(vw.dtype), vw)
    g = gates.reshape(B, Tq, N_KV, Q_PER_KV, 3)
    o = g[..., 0:1] * o_c + g[..., 1:2] * o_s + g[..., 2:3] * o_w
    return o.reshape(B, Tq, N_HEADS * HEAD_DIM)


def spatial_gate(u, v, lp):
    B, T, _ = v.shape
    vn = layer_norm(v, lp['gm_ln_g'], lp['gm_ln_b'])
    n_ch = -(-T // GM_CHUNK)
    vc = jnp.pad(vn, ((0, 0), (0, n_ch * GM_CHUNK - T), (0, 0))).reshape(B, n_ch, GM_CHUNK, GM_GROUPS, GM_DIM // GM_GROUPS)
    w = lp['gm_ws'] * jnp.tril(jnp.ones((GM_CHUNK, GM_CHUNK), lp['gm_ws'].dtype))
    mixed = jnp.einsum('gij,bcjgd->bcigd', w, vc) + lp['gm_bs'].T[None, None, :, :, None]
    mixed = mixed.reshape(B, n_ch * GM_CHUNK, GM_DIM)[:, :T]
    return (u * mixed) @ lp['w_up_gm'], vn


def hier_moe(h, lp):
    g_logit = (h @ lp['router_g_w'] + lp['router_g_b']).astype(jnp.float32)
    g_sel = jnp.argmax(g_logit, axis=-1)
    p_grp = jnp.take_along_axis(jax.nn.softmax(g_logit, axis=-1), g_sel[:, None], axis=-1)
    e_logit = (jnp.einsum('nd,dge->nge', h, lp['router_e_w']) + lp['router_e_b']).astype(jnp.float32)
    e_logit = jnp.take_along_axis(e_logit, g_sel[:, None, None], axis=1)[:, 0]
    top_v, top_i = lax.top_k(e_logit, EXPERT_TOPK)
    p = jax.nn.softmax(top_v, axis=-1) * p_grp
    eid = g_sel[:, None] * EXPERTS_PER_GROUP + top_i
    wt = jnp.einsum('nk,nke->ne', p, jax.nn.one_hot(eid, N_EXPERTS, dtype=jnp.float32)).astype(h.dtype)
    a = jnp.einsum('nd,edf->nef', h, lp['moe_w_gate'])
    b = jnp.einsum('nd,edf->nef', h, lp['moe_w_up'])
    hid = jax.nn.silu(a) * b * wt[:, :, None]
    return jnp.einsum('nef,efd->nd', hid, lp['moe_w_down'])


def finish_layer(x, ya, o_nsa, yc, g_m, lp):
    yb = o_nsa @ lp['w_up_nsa']
    mix = (g_m[:, :, 0] * ya + g_m[:, :, 1] * yb + g_m[:, :, 2] * yc) @ lp['w_o']
    h = layer_norm(DN_ALPHA * x + mix, lp['ln_g'][0], lp['ln_b'][0])
    B, T, D = h.shape
    f = hier_moe(h.reshape(B * T, D), lp).reshape(B, T, D)
    return layer_norm(DN_ALPHA * h + f, lp['ln_g'][1], lp['ln_b'][1])


def prompt_layer(x, lp):
    B, S, _ = x.shape
    glu, q, kv, g_nsa, u, v, g_m = mixer_inputs(x, lp['w_in'], lp['b_in'])
    ya = conv_module(jnp.pad(glu, ((0, 0), (CONV_WIDTH - 1, 0), (0, 0))), lp)
    kc, vc, ks, vs = nsa_keys(kv[:, :, :N_KV_SLOTS], lp)
    kw_pad = jnp.pad(kv[:, :, 4], ((0, 0), (WINDOW, 0), (0, 0), (0, 0)))
    vw_pad = jnp.pad(kv[:, :, 5], ((0, 0), (WINDOW, 0), (0, 0), (0, 0)))
    nqb = S // Q_BLOCK
    qb = q.reshape(B, nqb, Q_BLOCK, N_HEADS, HEAD_DIM).swapaxes(0, 1)
    gb = g_nsa.reshape(B, nqb, Q_BLOCK, N_HEADS, 3).swapaxes(0, 1)

    def body(args):
        qi, gi, i = args
        s0 = i * Q_BLOCK
        pos = s0 + jnp.arange(Q_BLOCK)
        kw = lax.dynamic_slice_in_dim(kw_pad, s0, WINDOW + Q_BLOCK, axis=1)
        vw = lax.dynamic_slice_in_dim(vw_pad, s0, WINDOW + Q_BLOCK, axis=1)
        pos_w = s0 - WINDOW + jnp.arange(WINDOW + Q_BLOCK)
        return nsa_block(qi, pos, gi, kc, vc, ks, vs, kw, vw, pos_w)

    o_nsa = lax.map(body, (qb, gb, jnp.arange(nqb))).swapaxes(0, 1).reshape(B, S, N_HEADS * HEAD_DIM)
    yc, _ = spatial_gate(u, v, lp)
    y = finish_layer(x, ya, o_nsa, yc, g_m, lp)
    wb = min(WINDOW, S)
    return y, kv[:, :, :N_KV_SLOTS], kv[:, S - wb:, 4:], glu[:, S - (CONV_WIDTH - 1):]


def sample_layer(x, kv_past, win_past, conv_past, lp):
    T = x.shape[1]
    P = kv_past.shape[1]
    WB = win_past.shape[1]
    glu, q, kv, g_nsa, u, v, g_m = mixer_inputs(x, lp['w_in'], lp['b_in'])
    conv_ctx = jnp.concatenate([conv_past, glu], axis=1)
    ya = conv_module(conv_ctx, lp)
    full = jnp.concatenate([kv_past, kv[:, :, :N_KV_SLOTS]], axis=1)
    win = jnp.concatenate([win_past, kv[:, :, 4:]], axis=1)
    kc, vc, ks, vs = nsa_keys(full, lp)
    pos = P + jnp.arange(T)
    pos_w = P - WB + jnp.arange(WB + T)
    o_nsa = nsa_block(q, pos, g_nsa, kc, vc, ks, vs, win[:, :, 0], win[:, :, 1], pos_w)
    yc, vn = spatial_gate(u, v, lp)
    y = finish_layer(x, ya, o_nsa, yc, g_m, lp)
    return y, kv[:, :, :N_KV_SLOTS], win[:, T:], conv_ctx[:, T:], vn


def setup_inputs(seed: int = 0) -> dict:
    key = jax.random.key(seed)
    keys = iter(jax.random.split(key, 40))

    def nrm(shape, s):
        return s * jax.random.normal(next(keys), shape, jnp.float32)

    n_pages = PAST_LEN // PAGE_SIZE
    n_pool = (DEC_BATCH * n_pages * 5) // 4
    win_buf = min(WINDOW, PAST_LEN)
    perm = jax.random.permutation(next(keys), n_pool)
    page_table = perm[:DEC_BATCH * n_pages].reshape(DEC_BATCH, n_pages).astype(jnp.int32)
    L = DEPTH
    return {
        'x_prompt': nrm((BATCH, SEQ, D_MODEL), 1.0),
        'x_sample': nrm((DEC_BATCH, DEC_SEQ, D_MODEL), 1.0),
        'cache_kv': nrm((L, n_pool, PAGE_SIZE, N_KV_SLOTS, N_KV, HEAD_DIM), 1.0),
        'state_win': nrm((L, DEC_BATCH, win_buf, 2, N_KV, HEAD_DIM), 1.0),
        'state_conv': nrm((L, DEC_BATCH, CONV_WIDTH - 1, CONV_DIM), 0.5),
        'page_table': page_table,
        'w_in': nrm((L, D_MODEL, IN_DIM), D_MODEL ** -0.5),
        'b_in': nrm((L, IN_DIM), 0.02),
        'conv_w': nrm((L, CONV_WIDTH, CONV_DIM), CONV_WIDTH ** -0.5),
        'conv_b': nrm((L, CONV_DIM), 0.02),
        'conv_ln_g': 1.0 + nrm((L, CONV_DIM), 0.05),
        'conv_ln_b': nrm((L, CONV_DIM), 0.02),
        'w_up_conv': nrm((L, CONV_DIM, D_MODEL), DN_BETA * CONV_DIM ** -0.5),
        'cmp_pe': nrm((L, 2, CMP_BLOCK, HEAD_DIM), 0.1),
        'cmp_w': nrm((L, 2, CMP_BLOCK, HEAD_DIM, HEAD_DIM), (CMP_BLOCK * HEAD_DIM) ** -0.5),
        'w_up_nsa': nrm((L, N_HEADS * HEAD_DIM, D_MODEL), DN_BETA * (N_HEADS * HEAD_DIM) ** -0.5),
        'gm_ln_g': 1.0 + nrm((L, GM_DIM), 0.05),
        'gm_ln_b': nrm((L, GM_DIM), 0.02),
        'gm_ws': nrm((L, GM_GROUPS, GM_CHUNK, GM_CHUNK), GM_CHUNK ** -0.5),
        'gm_bs': 1.0 + nrm((L, GM_GROUPS, GM_CHUNK), 0.1),
        'w_up_gm': nrm((L, GM_DIM, D_MODEL), DN_BETA * GM_DIM ** -0.5),
        'w_o': nrm((L, D_MODEL, D_MODEL), DN_BETA * D_MODEL ** -0.5),
        'ln_g': 1.0 + nrm((L, 2, D_MODEL), 0.05),
        'ln_b': nrm((L, 2, D_MODEL), 0.02),
        'router_g_w': nrm((L, D_MODEL, N_GROUPS), D_MODEL ** -0.5),
        'router_g_b': nrm((L, N_GROUPS), 0.01),
        'router_e_w': nrm((L, D_MODEL, N_GROUPS, EXPERTS_PER_GROUP), D_MODEL ** -0.5),
        'router_e_b': nrm((L, N_GROUPS, EXPERTS_PER_GROUP), 0.01),
        'moe_w_gate': nrm((L, N_EXPERTS, D_MODEL, EXPERT_HIDDEN), D_MODEL ** -0.5),
        'moe_w_up': nrm((L, N_EXPERTS, D_MODEL, EXPERT_HIDDEN), D_MODEL ** -0.5),
        'moe_w_down': nrm((L, N_EXPERTS, EXPERT_HIDDEN, D_MODEL), DN_BETA * EXPERT_HIDDEN ** -0.5),
    }


def reference(x_prompt, x_sample, cache_kv, state_win, state_conv, page_table, w_in, b_in, conv_w, conv_b,
              conv_ln_g, conv_ln_b, w_up_conv, cmp_pe, cmp_w, w_up_nsa, gm_ln_g, gm_ln_b, gm_ws, gm_bs, w_up_gm,
              w_o, ln_g, ln_b, router_g_w, router_g_b, router_e_w, router_e_b, moe_w_gate, moe_w_up, moe_w_down):
    dec_b, n_pages = page_table.shape
    past = n_pages * PAGE_SIZE
    xp, xs = x_prompt, x_sample
    kvp, kvs, wnp, wns, cvp, cvs, gmv = [], [], [], [], [], [], []
    for l in range(DEPTH):
        lp = dict(w_in=w_in[l], b_in=b_in[l], conv_w=conv_w[l], conv_b=conv_b[l], conv_ln_g=conv_ln_g[l],
                  conv_ln_b=conv_ln_b[l], w_up_conv=w_up_conv[l], cmp_pe=cmp_pe[l], cmp_w=cmp_w[l],
                  w_up_nsa=w_up_nsa[l], gm_ln_g=gm_ln_g[l], gm_ln_b=gm_ln_b[l], gm_ws=gm_ws[l], gm_bs=gm_bs[l],
                  w_up_gm=w_up_gm[l], w_o=w_o[l], ln_g=ln_g[l], ln_b=ln_b[l], router_g_w=router_g_w[l],
                  router_g_b=router_g_b[l], router_e_w=router_e_w[l], router_e_b=router_e_b[l],
                  moe_w_gate=moe_w_gate[l], moe_w_up=moe_w_up[l], moe_w_down=moe_w_down[l])
        xp, kv_rows, win_rows, conv_rows = prompt_layer(xp, lp)
        kvp.append(kv_rows); wnp.append(win_rows); cvp.append(conv_rows)
        kv_past = cache_kv[l][page_table].reshape(dec_b, past, N_KV_SLOTS, N_KV, HEAD_DIM)
        xs, kv_new, win_new, conv_new, v_new = sample_layer(xs, kv_past, state_win[l], state_conv[l], lp)
        kvs.append(kv_new); wns.append(win_new); cvs.append(conv_new); gmv.append(v_new)
    return (xp, xs, jnp.stack(kvp), jnp.stack(kvs), jnp.stack(wnp), jnp.stack(wns), jnp.stack(cvp), jnp.stack(cvs), jnp.stack(gmv))
```

```python
import functools

import jax
import jax.numpy as jnp
import numpy as np
from jax import lax
from jax.experimental import pallas as pl
from jax.experimental.pallas import tpu as pltpu

F32 = jnp.float32
BF16 = jnp.bfloat16

D_MODEL = 1024
CONV_WIDTH = 31
N_HEADS = 16
N_KV = 4
HEAD_DIM = 64
Q_PER_KV = N_HEADS // N_KV
CMP_BLOCK = 32
CMP_STRIDE = 16
CMP_SUB = CMP_BLOCK // CMP_STRIDE
SEL_BLOCK = 64
SEL_SHIFT = 6
SEL_TOPN = 8
WINDOW = 512
PAGE_SIZE = 128
N_KV_SLOTS = 4
FORCE_BONUS = 1.0e4
GM_GROUPS = 4
GM_CHUNK = 128
N_GROUPS = 4
EXPERTS_PER_GROUP = 4
EPG_SHIFT = 2
N_EXPERTS = N_GROUPS * EXPERTS_PER_GROUP
EXPERT_HIDDEN = 512
LN_EPS = 1e-5
NEG = -1e30
KV_W = N_KV * HEAD_DIM
KV_COLS = 6 * KV_W
GATE_PAD = 128

V7X_VMEM_BYTES = 64 * 1024 * 1024
VMEM_LIMIT = V7X_VMEM_BYTES * 7 // 8
LANES = 128
SUBLANES = 8

HIGHEST = lax.Precision.HIGHEST


def _cparams(sem):
    return pltpu.CompilerParams(dimension_semantics=sem, vmem_limit_bytes=VMEM_LIMIT)


def _pick(n, cands):
    for c in cands:
        if n % c == 0:
            return c
    raise ValueError(f"no tile in {cands} divides {n}")


def _sigmoid(x):
    return 1.0 / (1.0 + jnp.exp(-x))


def _gelu_tanh(x):
    return 0.5 * x * (1.0 + jnp.tanh(np.sqrt(2.0 / np.pi).astype(np.float32) * (x + 0.044715 * (x * x * x))))


def _layer_norm(x, g, b):
    mu = jnp.mean(x, axis=-1, keepdims=True)
    xc = x - mu
    var = jnp.mean(xc * xc, axis=-1, keepdims=True)
    return xc * lax.rsqrt(var + LN_EPS) * g + b


def _dot(a, b):
    return jnp.dot(a, b, preferred_element_type=F32)


def _dot_nt(a, b, precision=None):
    return lax.dot_general(a, b, (((1,), (1,)), ((), ())), preferred_element_type=F32, precision=precision)


def _masked_softmax_rows(s, mask):
    s = jnp.where(mask, s, NEG)
    m = jnp.max(s, axis=-1, keepdims=True)
    e = jnp.exp(s - m)
    p = e / jnp.sum(e, axis=-1, keepdims=True)
    return jnp.where(mask, p, 0.0)


_C_A = 0
_C_GATE = _C_A + D_MODEL
_C_Q = _C_GATE + D_MODEL
_C_KV = _C_Q + D_MODEL
_C_U = _C_KV + KV_COLS
_C_V = _C_U + D_MODEL
_C_GM = _C_V + D_MODEL
_C_GN = _C_GM + 3 * D_MODEL
IN_COLS = _C_GN + GATE_PAD


def _in_proj_kernel(x_ref, w_ref, b_ref, glu_ref, q_ref, kvf_ref, kvh_ref, u_ref, v_ref, gm_ref, gn_ref):
    x = x_ref[...].astype(BF16)

    def seg(lo, hi):
        return _dot(x, w_ref[:, lo:hi]) + b_ref[:, lo:hi]

    glu_ref[...] = seg(_C_A, _C_GATE) * _sigmoid(seg(_C_GATE, _C_Q))
    q_ref[...] = (seg(_C_Q, _C_KV) * (HEAD_DIM ** -0.5)).astype(BF16)
    kv = seg(_C_KV, _C_U)
    kvf_ref[...] = kv
    kvh_ref[...] = kv[:, 2 * KV_W:].astype(BF16)
    u_ref[...] = _gelu_tanh(seg(_C_U, _C_V)).astype(BF16)
    v_ref[...] = _gelu_tanh(seg(_C_V, _C_GM))
    gm_ref[...] = _sigmoid(seg(_C_GM, _C_GN))
    gn_ref[...] = _sigmoid(seg(_C_GN, IN_COLS))


def _in_proj(x, w_r, b_r):
    n = x.shape[0]
    tm = _pick(n, (256, 128, 64, 32, 16, 8))
    row = lambda w: pl.BlockSpec((tm, w), lambda i: (i, 0))
    return pl.pallas_call(
        _in_proj_kernel,
        grid=(n // tm,),
        in_specs=[row(D_MODEL),
                  pl.BlockSpec((D_MODEL, IN_COLS), lambda i: (0, 0), pipeline_mode=pl.Buffered(1)),
                  pl.BlockSpec((1, IN_COLS), lambda i: (0, 0))],
        out_specs=[row(D_MODEL), row(D_MODEL), row(KV_COLS), row(4 * KV_W), row(D_MODEL), row(D_MODEL),
                   row(3 * D_MODEL), row(GATE_PAD)],
        out_shape=[jax.ShapeDtypeStruct((n, D_MODEL), F32),
                   jax.ShapeDtypeStruct((n, D_MODEL), BF16),
                   jax.ShapeDtypeStruct((n, KV_COLS), F32),
                   jax.ShapeDtypeStruct((n, 4 * KV_W), BF16),
                   jax.ShapeDtypeStruct((n, D_MODEL), BF16),
                   jax.ShapeDtypeStruct((n, D_MODEL), F32),
                   jax.ShapeDtypeStruct((n, 3 * D_MODEL), F32),
                   jax.ShapeDtypeStruct((n, GATE_PAD), F32)],
        compiler_params=_cparams(("parallel",)),
        name="in_proj",
    )(x, w_r, b_r)


_CONV_HALO = 32
_CONV_ROWS = SUBLANES
_CONV_SPAN = 40


def _conv_prompt_kernel(halo_ref, glu_ref, w_ref, cb_ref, g_ref, b_ref, out_ref, win_ref, *, tq):
    i = pl.program_id(1)
    win_ref[0:_CONV_HALO, :] = jnp.where(i > 0, halo_ref[...], 0.0)
    win_ref[_CONV_HALO:_CONV_HALO + tq, :] = glu_ref[...]
    lead = _CONV_HALO - (CONV_WIDTH - 1)

    def chunk(c, carry):
        r0 = pl.multiple_of(c * _CONV_ROWS, _CONV_ROWS)
        x = win_ref[pl.ds(r0, _CONV_SPAN), :]
        acc = jnp.zeros((_CONV_ROWS, D_MODEL), F32)
        for k in range(CONV_WIDTH):
            acc = acc + x[lead + k:lead + k + _CONV_ROWS, :] * w_ref[k:k + 1, :]
        y = _layer_norm(acc + cb_ref[...], g_ref[...], b_ref[...])
        out_ref[pl.ds(r0, _CONV_ROWS), :] = (y * _sigmoid(y)).astype(BF16)
        return carry

    lax.fori_loop(0, tq // _CONV_ROWS, chunk, 0)


def _conv_prompt(glu, conv_w, conv_b, ln_g, ln_b, batch, seq):
    tq = _pick(seq, (256, 128))
    glu3 = glu.reshape(batch, seq, D_MODEL)
    hb = tq // _CONV_HALO
    vec = pl.BlockSpec((1, D_MODEL), lambda b, i: (0, 0))
    out = pl.pallas_call(
        functools.partial(_conv_prompt_kernel, tq=tq),
        grid=(batch, seq // tq),
        in_specs=[pl.BlockSpec((None, _CONV_HALO, D_MODEL), lambda b, i: (b, jnp.maximum(i * hb - 1, 0), 0)),
                  pl.BlockSpec((None, tq, D_MODEL), lambda b, i: (b, i, 0)),
                  pl.BlockSpec((CONV_WIDTH, D_MODEL), lambda b, i: (0, 0)),
                  vec, vec, vec],
        out_specs=pl.BlockSpec((None, tq, D_MODEL), lambda b, i: (b, i, 0)),
        out_shape=jax.ShapeDtypeStruct((batch, seq, D_MODEL), BF16),
        scratch_shapes=[pltpu.VMEM((_CONV_HALO + tq, D_MODEL), F32)],
        compiler_params=_cparams(("parallel", "parallel")),
        name="conv_prompt",
    )(glu3, glu3, conv_w, conv_b[None], ln_g[None], ln_b[None])
    return out.reshape(batch * seq, D_MODEL)


def _gmlp_prompt_kernel(u_ref, v_ref, g_ref, b_ref, ws_ref, bs_ref, out_ref):
    vn = _layer_norm(v_ref[...], g_ref[...], b_ref[...]).astype(BF16)
    ri = lax.broadcasted_iota(jnp.int32, (GM_CHUNK, GM_CHUNK), 0)
    ci = lax.broadcasted_iota(jnp.int32, (GM_CHUNK, GM_CHUNK), 1)
    gw = D_MODEL // GM_GROUPS
    for g in range(GM_GROUPS):
        w = jnp.where(ri >= ci, ws_ref[g], 0.0).astype(BF16)
        mixed = _dot(w, vn[:, g * gw:(g + 1) * gw]) + bs_ref[:, g:g + 1]
        out_ref[:, g * gw:(g + 1) * gw] = (u_ref[:, g * gw:(g + 1) * gw].astype(F32) * mixed).astype(BF16)


def _gmlp_prompt(u, v, ln_g, ln_b, gm_ws, gm_bs):
    n = u.shape[0]
    vec = pl.BlockSpec((1, D_MODEL), lambda i: (0, 0))
    row = pl.BlockSpec((GM_CHUNK, D_MODEL), lambda i: (i, 0))
    return pl.pallas_call(
        _gmlp_prompt_kernel,
        grid=(n // GM_CHUNK,),
        in_specs=[row, row, vec, vec,
                  pl.BlockSpec((GM_GROUPS, GM_CHUNK, GM_CHUNK), lambda i: (0, 0, 0)),
                  pl.BlockSpec((GM_CHUNK, GM_GROUPS), lambda i: (0, 0))],
        out_specs=row,
        out_shape=jax.ShapeDtypeStruct((n, D_MODEL), BF16),
        compiler_params=_cparams(("parallel",)),
        name="gmlp_prompt",
    )(u, v, ln_g[None], ln_b[None], gm_ws, gm_bs.T)


def _sample_seq_kernel(ctx_ref, u_ref, v_ref, cw_ref, cb_ref, cg_ref, cbb_ref, gg_ref, gb_ref, wrow_ref, bsrow_ref,
                       ca_ref, gc_ref, vn_ref, *, t_new):
    for t in range(t_new):
        acc = ctx_ref[t] * cw_ref[0:1, :]
        for k in range(1, CONV_WIDTH):
            acc = acc + ctx_ref[t + k] * cw_ref[k:k + 1, :]
        y = _layer_norm(acc + cb_ref[...], cg_ref[...], cbb_ref[...])
        ca_ref[t] = (y * _sigmoid(y)).astype(BF16)
    for t in range(t_new):
        vn_ref[t] = _layer_norm(v_ref[t], gg_ref[...], gb_ref[...])
    for i in range(t_new):
        mixed = bsrow_ref[i:i + 1, :]
        for j in range(i + 1):
            mixed = mixed + wrow_ref[i * t_new + j:i * t_new + j + 1, :] * vn_ref[j]
        gc_ref[i] = (u_ref[i].astype(F32) * mixed).astype(BF16)


def _sample_seq(ctx_t, u_t, v_t, conv_w, conv_b, cg, cb, gg, gb, wrow, bsrow):
    rows, dec_b, _ = ctx_t.shape
    t_new = u_t.shape[0]
    sb = _pick(dec_b, (32, 16, 8))
    vec = pl.BlockSpec((1, D_MODEL), lambda i: (0, 0))
    blk = lambda r: pl.BlockSpec((r, sb, D_MODEL), lambda i: (0, i, 0))
    full = lambda r: pl.BlockSpec((r, D_MODEL), lambda i: (0, 0))
    return pl.pallas_call(
        functools.partial(_sample_seq_kernel, t_new=t_new),
        grid=(dec_b // sb,),
        in_specs=[blk(rows), blk(t_new), blk(t_new), full(CONV_WIDTH), vec, vec, vec, vec, vec,
                  full(t_new * t_new), full(t_new)],
        out_specs=[blk(t_new), blk(t_new), blk(t_new)],
        out_shape=[jax.ShapeDtypeStruct((t_new, dec_b, D_MODEL), BF16),
                   jax.ShapeDtypeStruct((t_new, dec_b, D_MODEL), BF16),
                   jax.ShapeDtypeStruct((t_new, dec_b, D_MODEL), F32)],
        compiler_params=_cparams(("parallel",)),
        name="sample_seq",
    )(ctx_t, u_t, v_t, conv_w, conv_b[None], cg[None], cb[None], gg[None], gb[None], wrow, bsrow)


_CMP_K = CMP_STRIDE * KV_W
_CMP_N = CMP_SUB * KV_W


def _combine_halves(y, bias):
    first = y[:, :KV_W] + bias[0:1, :KV_W]
    second = y[:, KV_W:] + bias[1:2, KV_W:]
    return first + pltpu.roll(second, shift=y.shape[0] - 1, axis=0)


def _cmp_prompt_kernel(lo_ref, hi_ref, pe_ref, w_ref, out_ref, *, n_chunks):
    bias = _dot(pe_ref[...].astype(BF16), w_ref[...])
    y = jnp.zeros((n_chunks, _CMP_N), F32)
    for l in range(CMP_STRIDE):
        rows = jnp.concatenate([lo_ref[pl.ds(l, n_chunks, stride=CMP_STRIDE), :],
                                hi_ref[pl.ds(l, n_chunks, stride=CMP_STRIDE), :]], axis=-1).astype(BF16)
        y = y + _dot(rows, w_ref[l * KV_W:(l + 1) * KV_W, :])
    out_ref[...] = _combine_halves(y, bias)


def _cmp_prompt(kvf, pe8, wbd, batch, seq):
    n_chunks = seq // CMP_STRIDE
    kv3 = kvf.reshape(batch, seq, KV_COLS)
    return pl.pallas_call(
        functools.partial(_cmp_prompt_kernel, n_chunks=n_chunks),
        grid=(batch, 2),
        in_specs=[pl.BlockSpec((None, seq, LANES), lambda b, s: (b, 0, 2 * s)),
                  pl.BlockSpec((None, seq, LANES), lambda b, s: (b, 0, 2 * s + 1)),
                  pl.BlockSpec((None, SUBLANES, _CMP_K), lambda b, s: (s, 0, 0)),
                  pl.BlockSpec((None, _CMP_K, _CMP_N), lambda b, s: (s, 0, 0))],
        out_specs=pl.BlockSpec((None, None, n_chunks, KV_W), lambda b, s: (b, s, 0, 0)),
        out_shape=jax.ShapeDtypeStruct((batch, 2, n_chunks, KV_W), F32),
        compiler_params=_cparams(("parallel", "parallel")),
        name="cmp_prompt",
    )(kv3, kv3, pe8, wbd)


def _cmp_page_copy(cache_ref, buf_ref, sem_ref, pt_ref, layer, sample, page, slot, k, n_pages):
    pid = pt_ref[sample * n_pages + page]
    return pltpu.make_async_copy(cache_ref.at[layer, pid, pl.ds(0, 2 * KV_W), :], buf_ref.at[slot, k], sem_ref.at[slot])


def _cmp_sample_kernel(pt_ref, perm_ref, pe_ref, w_ref, cache_ref, out_ref, buf_ref, rk_ref, rv_ref, sem_ref,
                       *, sb, n_pages, steps_per_layer, n_steps):
    step = pl.program_id(0)
    slot = step % 2
    per_step = sb * n_pages

    def issue(st, sl):
        layer = st // steps_per_layer
        s0 = (st % steps_per_layer) * sb

        def body(k, carry):
            _cmp_page_copy(cache_ref, buf_ref, sem_ref, pt_ref, layer, s0 + k // n_pages, k % n_pages, sl, k,
                           n_pages).start()
            return carry

        lax.fori_loop(0, per_step, body, 0)

    @pl.when(step == 0)
    def _():
        issue(step, slot)

    @pl.when(step + 1 < n_steps)
    def _():
        issue(step + 1, 1 - slot)

    def wait_body(k, carry):
        _cmp_page_copy(cache_ref, buf_ref, sem_ref, pt_ref, 0, 0, 0, slot, k, n_pages).wait()
        return carry

    lax.fori_loop(0, per_step, wait_body, 0)

    chunks_per_page = PAGE_SIZE // CMP_STRIDE

    def page_body(k, carry):
        a = buf_ref[slot, k].astype(BF16)
        t = _dot_nt(perm_ref[...], a)
        r0 = pl.multiple_of(k * chunks_per_page, chunks_per_page)
        for l in range(CMP_STRIDE):
            piece = t[l * chunks_per_page:(l + 1) * chunks_per_page, :]
            rk_ref[pl.ds(r0, chunks_per_page), l * KV_W:(l + 1) * KV_W] = piece[:, :KV_W]
            rv_ref[pl.ds(r0, chunks_per_page), l * KV_W:(l + 1) * KV_W] = piece[:, KV_W:]
        return carry

    lax.fori_loop(0, per_step, page_body, 0)

    for s, r_ref in ((0, rk_ref), (1, rv_ref)):
        bias = _dot(pe_ref[s].astype(BF16), w_ref[s])
        y = _dot(r_ref[...].astype(BF16), w_ref[s])
        out_ref[s] = _combine_halves(y, bias)


def _cmp_sample(page_table, cache_t, perm, pe8, wbd):
    depth = cache_t.shape[0]
    dec_b, n_pages = page_table.shape
    sb = _pick(dec_b, (2, 1))
    steps_per_layer = dec_b // sb
    n_steps = depth * steps_per_layer
    rows = sb * n_pages * (PAGE_SIZE // CMP_STRIDE)
    grid_spec = pltpu.PrefetchScalarGridSpec(
        num_scalar_prefetch=1,
        grid=(n_steps,),
        in_specs=[pl.BlockSpec((PAGE_SIZE, PAGE_SIZE), lambda i, pt: (0, 0)),
                  pl.BlockSpec((None, 2, SUBLANES, _CMP_K), lambda i, pt: (i // steps_per_layer, 0, 0, 0)),
                  pl.BlockSpec((None, 2, _CMP_K, _CMP_N), lambda i, pt: (i // steps_per_layer, 0, 0, 0)),
                  pl.BlockSpec(memory_space=pl.ANY)],
        out_specs=pl.BlockSpec((None, 2, rows, KV_W), lambda i, pt: (i, 0, 0, 0)),
        scratch_shapes=[pltpu.VMEM((2, sb * n_pages, 2 * KV_W, PAGE_SIZE), F32),
                        pltpu.VMEM((rows, _CMP_K), F32),
                        pltpu.VMEM((rows, _CMP_K), F32),
                        pltpu.SemaphoreType.DMA((2,))])
    out = pl.pallas_call(
        functools.partial(_cmp_sample_kernel, sb=sb, n_pages=n_pages, steps_per_layer=steps_per_layer,
                          n_steps=n_steps),
        grid_spec=grid_spec,
        out_shape=jax.ShapeDtypeStruct((n_steps, 2, rows, KV_W), F32),
        compiler_params=_cparams(("arbitrary",)),
        name="cmp_sample",
    )(page_table.reshape(-1), perm, pe8, wbd, cache_t)
    n_chunks = rows // sb
    out = out.reshape(depth, steps_per_layer, 2, sb, n_chunks, KV_W)
    return out.transpose(0, 1, 3, 2, 4, 5).reshape(depth, dec_b, 2, n_chunks, KV_W)


_TQ = 128
_KC = 256


def _rank_select(score, blk, n_blocks, axis):
    rank = jnp.zeros(score.shape, F32)
    for sp in range(n_blocks):
        row = lax.slice_in_dim(score, sp, sp + 1, axis=axis)
        before = jnp.where(blk > sp, 1.0, 0.0)
        rank = rank + jnp.where(row > score, 1.0, jnp.where(row == score, before, 0.0))
    return jnp.where(rank < float(min(SEL_TOPN, n_blocks)), 1.0, 0.0)


def _nsa_prompt_kernel(q_ref, gn_ref, kvc_ref, kv_ref, mt_ref, ex_ref, o_ref, mask_ref, *, seq, n_cmp, n_sel, n_sel_pad):
    i = pl.program_id(1)
    t0 = i * _TQ
    rows = Q_PER_KV * _TQ
    n_chunks = kvc_ref.shape[1]
    win_keys = min(WINDOW + _TQ, seq)

    pos = t0 + (lax.broadcasted_iota(jnp.int32, (rows, 1), 0) & (_TQ - 1))
    pos_t = t0 + lax.broadcasted_iota(jnp.int32, (1, _TQ), 1)

    n_idx = lax.broadcasted_iota(jnp.int32, (1, n_chunks), 1)
    mask_c = (n_idx * CMP_STRIDE + (CMP_BLOCK - 1) <= pos) & (n_idx < n_cmp)

    blk = lax.broadcasted_iota(jnp.int32, (n_sel_pad, _TQ), 0)
    cur = jnp.right_shift(pos_t, SEL_SHIFT)
    valid = (blk <= cur) & (blk < n_sel)
    forced = (blk == 0) | (blk == cur) | (blk == cur - 1)

    w_start = pl.multiple_of(jnp.maximum(t0 + _TQ - win_keys, 0), _TQ)
    wpos = w_start + lax.broadcasted_iota(jnp.int32, (1, win_keys), 1)
    mask_w = (wpos <= pos) & (wpos > pos - WINDOW)
    n_kc = (t0 + _TQ + _KC - 1) // _KC

    for g in range(N_KV):
        lo = g * HEAD_DIM
        qg = jnp.concatenate([q_ref[:, (g * Q_PER_KV + r) * HEAD_DIM:(g * Q_PER_KV + r + 1) * HEAD_DIM]
                              for r in range(Q_PER_KV)], axis=0)

        kc = kvc_ref[0, :, lo:lo + HEAD_DIM].astype(BF16)
        vc = kvc_ref[1, :, lo:lo + HEAD_DIM].astype(BF16)
        p_c = _masked_softmax_rows(_dot_nt(qg, kc), mask_c)
        o_c = _dot(p_c.astype(BF16), vc)

        psum = p_c[0:_TQ]
        for r in range(1, Q_PER_KV):
            psum = psum + p_c[r * _TQ:(r + 1) * _TQ]
        imp = _dot_nt(mt_ref[...], psum, precision=HIGHEST)
        score = jnp.where(valid, imp + jnp.where(forced, FORCE_BONUS, 0.0), -FORCE_BONUS)
        score = jnp.where(blk < n_sel, score, -jnp.inf)
        sel = _rank_select(score, blk, n_sel, axis=0)
        if n_sel_pad < LANES:
            sel = jnp.concatenate([sel, jnp.zeros((LANES - n_sel_pad, _TQ), F32)], axis=0)
        mask_ref[...] = _dot(sel.T.astype(BF16), ex_ref[...])

        def sel_chunk(c, carry):
            m, l, acc = carry
            k0 = pl.multiple_of(c * _KC, _KC)
            ks = kv_ref[pl.ds(k0, _KC), lo:lo + HEAD_DIM]
            vs = kv_ref[pl.ds(k0, _KC), KV_W + lo:KV_W + lo + HEAD_DIM]
            s = _dot_nt(qg, ks)
            mk = mask_ref[:, pl.ds(k0, _KC)] > 0.5
            mk = jnp.concatenate([mk] * Q_PER_KV, axis=0)
            kpos = k0 + lax.broadcasted_iota(jnp.int32, (1, _KC), 1)
            mk = mk & (kpos <= pos)
            s = jnp.where(mk, s, NEG)
            m_new = jnp.maximum(m, jnp.max(s, axis=-1, keepdims=True))
            alpha = jnp.exp(m - m_new)
            p = jnp.where(mk, jnp.exp(s - m_new), 0.0)
            l = alpha * l + jnp.sum(p, axis=-1, keepdims=True)
            acc = alpha * acc + _dot(p.astype(BF16), vs)
            return m_new, l, acc

        m0 = jnp.full((rows, 1), NEG, F32)
        l0 = jnp.zeros((rows, 1), F32)
        a0 = jnp.zeros((rows, HEAD_DIM), F32)
        _, l_s, acc_s = lax.fori_loop(0, n_kc, sel_chunk, (m0, l0, a0))
        o_s = acc_s * jnp.where(l_s > 0.0, 1.0 / l_s, 0.0)

        kw = kv_ref[pl.ds(w_start, win_keys), 2 * KV_W + lo:2 * KV_W + lo + HEAD_DIM]
        vw = kv_ref[pl.ds(w_start, win_keys), 3 * KV_W + lo:3 * KV_W + lo + HEAD_DIM]
        p_w = _masked_softmax_rows(_dot_nt(qg, kw), mask_w)
        o_w = _dot(p_w.astype(BF16), vw)

        for r in range(Q_PER_KV):
            h = g * Q_PER_KV + r
            sl = slice(r * _TQ, (r + 1) * _TQ)
            o = (gn_ref[:, 3 * h:3 * h + 1] * o_c[sl] + gn_ref[:, 3 * h + 1:3 * h + 2] * o_s[sl]
                 + gn_ref[:, 3 * h + 2:3 * h + 3] * o_w[sl])
            o_ref[:, h * HEAD_DIM:(h + 1) * HEAD_DIM] = o.astype(BF16)


def _nsa_prompt(q, gn, kvc, kvh, batch, seq):
    n_chunks = seq // CMP_STRIDE
    n_cmp = (seq - CMP_BLOCK) // CMP_STRIDE + 1
    n_sel = -(-seq // SEL_BLOCK)
    n_sel_pad = -(-n_sel // SUBLANES) * SUBLANES
    mt = _cmp_to_sel_t(n_chunks, n_cmp, n_sel, n_sel_pad)
    ex = _block_expand(LANES, seq)
    tile = lambda w: pl.BlockSpec((None, _TQ, w), lambda b, i: (b, i, 0))
    out = pl.pallas_call(
        functools.partial(_nsa_prompt_kernel, seq=seq, n_cmp=n_cmp, n_sel=n_sel, n_sel_pad=n_sel_pad),
        grid=(batch, seq // _TQ),
        in_specs=[tile(D_MODEL), tile(GATE_PAD),
                  pl.BlockSpec((None, 2, n_chunks, KV_W), lambda b, i: (b, 0, 0, 0)),
                  pl.BlockSpec((None, seq, 4 * KV_W), lambda b, i: (b, 0, 0)),
                  pl.BlockSpec((n_sel_pad, n_chunks), lambda b, i: (0, 0)),
                  pl.BlockSpec((LANES, seq), lambda b, i: (0, 0))],
        out_specs=tile(D_MODEL),
        out_shape=jax.ShapeDtypeStruct((batch, seq, D_MODEL), BF16),
        scratch_shapes=[pltpu.VMEM((_TQ, seq), F32)],
        compiler_params=_cparams(("parallel", "arbitrary")),
        name="nsa_prompt",
    )(q.reshape(batch, seq, D_MODEL), gn.reshape(batch, seq, GATE_PAD), kvc, kvh.reshape(batch, seq, 4 * KV_W), mt, ex)
    return out.reshape(batch * seq, D_MODEL)


def _cmp_to_sel_t(n_chunks, n_cmp, n_sel, n_sel_pad):
    cs = np.arange(n_chunks)[None, :] * CMP_STRIDE
    ss = np.arange(n_sel_pad)[:, None] * SEL_BLOCK
    m = (cs < ss + SEL_BLOCK) & (cs + CMP_BLOCK > ss)
    m = m & (np.arange(n_chunks)[None, :] < n_cmp) & (np.arange(n_sel_pad)[:, None] < n_sel)
    return jnp.asarray(m, F32)


def _block_expand(rows, keys):
    return jnp.asarray(np.arange(keys)[None, :] // SEL_BLOCK == np.arange(rows)[:, None], BF16)


_NEW_PAD = 128


def _slc_page_copy(cache_ref, buf_ref, sem_ref, pt_ref, layer, sample, page, slot, n_pages):
    pid = pt_ref[sample * n_pages + page]
    return pltpu.make_async_copy(cache_ref.at[layer, pid, pl.ds(2 * KV_W, 2 * KV_W), :],
                                 buf_ref.at[slot, :, pl.ds(page * PAGE_SIZE, PAGE_SIZE)], sem_ref.at[slot])


def _joint_softmax(s_a, m_a, s_b, m_b):
    mx = jnp.maximum(jnp.max(jnp.where(m_a, s_a, NEG), axis=-1, keepdims=True),
                     jnp.max(jnp.where(m_b, s_b, NEG), axis=-1, keepdims=True))
    e_a = jnp.where(m_a, jnp.exp(s_a - mx), 0.0)
    e_b = jnp.where(m_b, jnp.exp(s_b - mx), 0.0)
    den = jnp.sum(e_a, axis=-1, keepdims=True) + jnp.sum(e_b, axis=-1, keepdims=True)
    return e_a, e_b, jnp.where(den > 0.0, 1.0 / den, 0.0)


def _nsa_sample_kernel(pt_ref, qbd_ref, gate_ref, kvc_ref, new_ref, win_ref, mt_ref, ex_ref, rr_ref, meta_ref,
                       cache_ref, o_ref, buf_ref, sem_ref, *, layer, n_pages, t_new, n_cmp, n_sel, dec_b):
    smp = pl.program_id(0)
    slot = smp % 2
    past = n_pages * PAGE_SIZE
    rows = N_KV * t_new * Q_PER_KV
    win_rows = win_ref.shape[1]

    def issue(s, sl):
        for page in range(n_pages):
            _slc_page_copy(cache_ref, buf_ref, sem_ref, pt_ref, layer, s, page, sl, n_pages).start()

    @pl.when(smp == 0)
    def _():
        issue(smp, slot)

    @pl.when(smp + 1 < dec_b)
    def _():
        issue(smp + 1, 1 - slot)

    for page in range(n_pages):
        _slc_page_copy(cache_ref, buf_ref, sem_ref, pt_ref, layer, 0, page, slot, n_pages).wait()

    qbd = qbd_ref[...]
    tok = meta_ref[:, 0:1]
    grp = meta_ref[:, 1:2]
    pos = past + tok

    n_chunks = kvc_ref.shape[1]
    n_idx = lax.broadcasted_iota(jnp.int32, (1, n_chunks), 1)
    mask_c = (n_idx * CMP_STRIDE + (CMP_BLOCK - 1) <= pos) & (n_idx < n_cmp)
    p_c = _masked_softmax_rows(_dot_nt(qbd, kvc_ref[0].astype(BF16)), mask_c)
    o_c = _dot(p_c.astype(BF16), kvc_ref[1].astype(BF16))

    imp = jnp.dot(rr_ref[...], jnp.dot(p_c, mt_ref[...], precision=HIGHEST, preferred_element_type=F32),
                  precision=HIGHEST, preferred_element_type=F32)
    blk = lax.broadcasted_iota(jnp.int32, (rows, LANES), 1)
    cur = jnp.right_shift(pos, SEL_SHIFT)
    valid = (blk <= cur) & (blk < n_sel)
    forced = (blk == 0) | (blk == cur) | (blk == cur - 1)
    score = jnp.where(valid, imp + jnp.where(forced, FORCE_BONUS, 0.0), -FORCE_BONUS)
    score = jnp.where(blk < n_sel, score, -jnp.inf)
    sel = _rank_select(score, blk, n_sel, axis=1)
    mask_s = _dot(sel.astype(BF16), ex_ref[...]) > 0.5

    i_new = lax.broadcasted_iota(jnp.int32, (1, _NEW_PAD), 1)
    new_ok = (i_new < t_new) & (i_new <= tok)
    new_blk = past // SEL_BLOCK
    mask_sn = new_ok & (sel[:, new_blk:new_blk + 1] > 0.5)

    ks = buf_ref[slot, 0:KV_W, :].astype(BF16)
    vs = buf_ref[slot, KV_W:2 * KV_W, :].astype(BF16)
    s_s = _dot(qbd, ks)
    s_sn = _dot_nt(qbd, new_ref[:, 0:KV_W])
    e_s, e_sn, inv_s = _joint_softmax(s_s, mask_s, s_sn, mask_sn)
    o_s = (_dot_nt(e_s.astype(BF16), vs) + _dot(e_sn.astype(BF16), new_ref[:, KV_W:2 * KV_W])) * inv_s

    kw = win_ref[0:KV_W, :].astype(BF16)
    vw = win_ref[KV_W:2 * KV_W, :].astype(BF16)
    i_w = lax.broadcasted_iota(jnp.int32, (1, win_rows), 1)
    mask_w = i_w > tok + (win_rows - WINDOW)
    s_w = _dot(qbd, kw)
    s_wn = _dot_nt(qbd, new_ref[:, 2 * KV_W:3 * KV_W])
    e_w, e_wn, inv_w = _joint_softmax(s_w, mask_w, s_wn, new_ok)
    o_w = (_dot_nt(e_w.astype(BF16), vw) + _dot(e_wn.astype(BF16), new_ref[:, 3 * KV_W:4 * KV_W])) * inv_w

    o_all = gate_ref[:, 0:1] * o_c + gate_ref[:, 1:2] * o_s + gate_ref[:, 2:3] * o_w
    out = jnp.zeros((rows, HEAD_DIM), F32)
    for g in range(N_KV):
        out = out + jnp.where(grp == g, o_all[:, g * HEAD_DIM:(g + 1) * HEAD_DIM], 0.0)
    o_ref[...] = out


def _nsa_sample(page_table, qbd, gates, kvc, new_rows, win_t, cache_t, layer, t_new):
    dec_b, n_pages = page_table.shape
    past = n_pages * PAGE_SIZE
    rows = N_KV * t_new * Q_PER_KV
    n_chunks = kvc.shape[2]
    n_cmp = (past + t_new - CMP_BLOCK) // CMP_STRIDE + 1
    n_sel = -(-(past + t_new) // SEL_BLOCK)
    win_rows = win_t.shape[-1]
    mt = _cmp_to_sel_t(n_chunks, n_cmp, n_sel, LANES).T
    ex = _block_expand(LANES, past)
    rr = jnp.asarray(np.arange(rows)[:, None] // Q_PER_KV == np.arange(rows)[None, :] // Q_PER_KV, F32)
    meta_np = np.zeros((rows, LANES), np.int32)
    meta_np[:, 0] = (np.arange(rows) // Q_PER_KV) % t_new
    meta_np[:, 1] = np.arange(rows) // (t_new * Q_PER_KV)
    meta = jnp.asarray(meta_np)
    per = lambda *shape: pl.BlockSpec((None,) + shape, lambda s, pt: (s,) + (0,) * len(shape))
    const = lambda *shape: pl.BlockSpec(shape, lambda s, pt: (0,) * len(shape))
    grid_spec = pltpu.PrefetchScalarGridSpec(
        num_scalar_prefetch=1,
        grid=(dec_b,),
        in_specs=[per(rows, KV_W), per(rows, LANES), per(2, n_chunks, KV_W), per(_NEW_PAD, 4 * KV_W),
                  per(2 * KV_W, win_rows), const(n_chunks, LANES), const(LANES, past), const(rows, rows),
                  const(rows, LANES), pl.BlockSpec(memory_space=pl.ANY)],
        out_specs=per(rows, HEAD_DIM),
        scratch_shapes=[pltpu.VMEM((2, 2 * KV_W, past), F32), pltpu.SemaphoreType.DMA((2,))])
    return pl.pallas_call(
        functools.partial(_nsa_sample_kernel, layer=layer, n_pages=n_pages, t_new=t_new, n_cmp=n_cmp, n_sel=n_sel,
                          dec_b=dec_b),
        grid_spec=grid_spec,
        out_shape=jax.ShapeDtypeStruct((dec_b, rows, HEAD_DIM), F32),
        compiler_params=_cparams(("arbitrary",)),
        name="nsa_sample",
    )(page_table.reshape(-1), qbd, gates, kvc, new_rows, win_t, mt, ex, rr, meta, cache_t)


def _merge_kernel(x_ref, ca_ref, on_ref, gc_ref, gm_ref, wc_ref, wn_ref, wg_ref, wo_ref, g_ref, b_ref, h_ref, *, alpha):
    ya = _dot(ca_ref[...], wc_ref[...])
    yb = _dot(on_ref[...], wn_ref[...])
    yc = _dot(gc_ref[...], wg_ref[...])
    m = (gm_ref[:, 0:D_MODEL] * ya + gm_ref[:, D_MODEL:2 * D_MODEL] * yb + gm_ref[:, 2 * D_MODEL:3 * D_MODEL] * yc)
    mix = _dot(m.astype(BF16), wo_ref[...])
    h_ref[...] = _layer_norm(alpha * x_ref[...] + mix, g_ref[...], b_ref[...])


def _merge(x, ca, on, gc, gm, wc, wn, wg, wo, ln_g, ln_b, alpha):
    n = x.shape[0]
    tm = _pick(n, (256, 128, 64, 32, 16, 8))
    row = lambda w: pl.BlockSpec((tm, w), lambda i: (i, 0))
    wspec = pl.BlockSpec((D_MODEL, D_MODEL), lambda i: (0, 0), pipeline_mode=pl.Buffered(1))
    vec = pl.BlockSpec((1, D_MODEL), lambda i: (0, 0))
    return pl.pallas_call(
        functools.partial(_merge_kernel, alpha=alpha),
        grid=(n // tm,),
        in_specs=[row(D_MODEL), row(D_MODEL), row(D_MODEL), row(D_MODEL), row(3 * D_MODEL),
                  wspec, wspec, wspec, wspec, vec, vec],
        out_specs=row(D_MODEL),
        out_shape=jax.ShapeDtypeStruct((n, D_MODEL), F32),
        compiler_params=_cparams(("parallel",)),
        name="merge",
    )(x, ca, on, gc, gm, wc, wn, wg, wo, ln_g[None], ln_b[None])


_ROUTER_PAD = 128


def _moe_kernel(h_ref, wr_ref, br_ref, wg_ref, wu_ref, wd_ref, g_ref, b_ref, o_ref, hb_ref, wt_ref, acc_ref, *, alpha):
    e = pl.program_id(1)
    lane = lax.broadcasted_iota(jnp.int32, (1, _ROUTER_PAD), 1)

    @pl.when(e == 0)
    def _():
        h = h_ref[...]
        hb_ref[...] = h.astype(BF16)
        logit = jnp.dot(h, wr_ref[...], precision=HIGHEST, preferred_element_type=F32) + br_ref[...]
        gl = jnp.where(lane < N_GROUPS, logit, -jnp.inf)
        gmax = jnp.max(gl, axis=-1, keepdims=True)
        g_sel = jnp.min(jnp.where(gl == gmax, lane, _ROUTER_PAD), axis=-1, keepdims=True)
        p_grp = 1.0 / jnp.sum(jnp.exp(gl - gmax), axis=-1, keepdims=True)
        lane_grp = jnp.right_shift(lane - N_GROUPS, EPG_SHIFT)
        in_grp = (lane >= N_GROUPS) & (lane < N_GROUPS + N_EXPERTS) & (lane_grp == g_sel)
        el = jnp.where(in_grp, logit, -jnp.inf)
        v1 = jnp.max(el, axis=-1, keepdims=True)
        i1 = jnp.min(jnp.where(el == v1, lane, _ROUTER_PAD), axis=-1, keepdims=True)
        el2 = jnp.where(lane == i1, -jnp.inf, el)
        v2 = jnp.max(el2, axis=-1, keepdims=True)
        i2 = jnp.min(jnp.where(el2 == v2, lane, _ROUTER_PAD), axis=-1, keepdims=True)
        e2 = jnp.exp(v2 - v1)
        p1 = 1.0 / (1.0 + e2)
        p2 = e2 / (1.0 + e2)
        wt_ref[...] = jnp.where(lane == i1, p1, jnp.where(lane == i2, p2, 0.0)) * p_grp
        acc_ref[...] = jnp.zeros_like(acc_ref)

    wt = jnp.sum(jnp.where(lane == N_GROUPS + e, wt_ref[...], 0.0), axis=-1, keepdims=True)
    hb = hb_ref[...]
    a = _dot(hb, wg_ref[...])
    b = _dot(hb, wu_ref[...])
    hid = (a * _sigmoid(a)) * b * wt
    acc_ref[...] += _dot(hid.astype(BF16), wd_ref[...])

    @pl.when(e == N_EXPERTS - 1)
    def _():
        o_ref[...] = _layer_norm(alpha * h_ref[...] + acc_ref[...], g_ref[...], b_ref[...])


def _moe(h, wr, br, wg, wu, wd, ln_g, ln_b, alpha):
    n = h.shape[0]
    tm = _pick(n, (512, 256, 128, 64, 32, 16, 8))
    row = pl.BlockSpec((tm, D_MODEL), lambda i, e: (i, 0))
    vec = pl.BlockSpec((1, D_MODEL), lambda i, e: (0, 0))
    return pl.pallas_call(
        functools.partial(_moe_kernel, alpha=alpha),
        grid=(n // tm, N_EXPERTS),
        in_specs=[row,
                  pl.BlockSpec((D_MODEL, _ROUTER_PAD), lambda i, e: (0, 0)),
                  pl.BlockSpec((1, _ROUTER_PAD), lambda i, e: (0, 0)),
                  pl.BlockSpec((None, D_MODEL, EXPERT_HIDDEN), lambda i, e: (e, 0, 0)),
                  pl.BlockSpec((None, D_MODEL, EXPERT_HIDDEN), lambda i, e: (e, 0, 0)),
                  pl.BlockSpec((None, EXPERT_HIDDEN, D_MODEL), lambda i, e: (e, 0, 0)),
                  vec, vec],
        out_specs=row,
        out_shape=jax.ShapeDtypeStruct((n, D_MODEL), F32),
        scratch_shapes=[pltpu.VMEM((tm, D_MODEL), BF16), pltpu.VMEM((tm, _ROUTER_PAD), F32),
                        pltpu.VMEM((tm, D_MODEL), F32)],
        compiler_params=_cparams(("parallel", "arbitrary")),
        name="moe",
    )(h, wr, br, wg, wu, wd, ln_g[None], ln_b[None])


def _prep_in_proj(w_in, b_in):
    c_gn = 2 * D_MODEL + D_MODEL + KV_COLS
    n_gn = 3 * N_HEADS
    pad = GATE_PAD - n_gn

    def rearr(a):
        z = jnp.zeros(a.shape[:-1] + (pad,), a.dtype)
        return jnp.concatenate([a[..., :c_gn], a[..., c_gn + n_gn:], a[..., c_gn:c_gn + n_gn], z], axis=-1)

    return rearr(w_in).astype(BF16), rearr(b_in)[:, None, :]


def _prep_cmp(cmp_pe, cmp_w):
    depth = cmp_w.shape[0]
    w = cmp_w.reshape(depth, 2, CMP_SUB, CMP_STRIDE, HEAD_DIM, HEAD_DIM)
    eye = jnp.eye(N_KV, dtype=cmp_w.dtype)
    wbd = jnp.einsum('zshlde,gk->zslgdhke', w, eye).reshape(depth, 2, _CMP_K, _CMP_N).astype(BF16)
    pe = cmp_pe.reshape(depth, 2, CMP_SUB, CMP_STRIDE, 1, HEAD_DIM)
    pe = jnp.broadcast_to(pe, (depth, 2, CMP_SUB, CMP_STRIDE, N_KV, HEAD_DIM)).reshape(depth, 2, CMP_SUB, _CMP_K)
    pe8 = jnp.concatenate([pe, jnp.zeros((depth, 2, SUBLANES - CMP_SUB, _CMP_K), pe.dtype)], axis=2)
    return pe8, wbd


def _deinterleave_perm():
    cpp = PAGE_SIZE // CMP_STRIDE
    r = np.arange(PAGE_SIZE)
    src = (r % cpp) * CMP_STRIDE + r // cpp
    return jnp.asarray(src[:, None] == np.arange(PAGE_SIZE)[None, :], BF16)


def kernel(x_prompt, x_sample, cache_kv, state_win, state_conv, page_table, w_in, b_in, conv_w, conv_b, conv_ln_g, conv_ln_b, w_up_conv, cmp_pe, cmp_w, w_up_nsa, gm_ln_g, gm_ln_b, gm_ws, gm_bs, w_up_gm, w_o, ln_g, ln_b, router_g_w, router_g_b, router_e_w, router_e_b, moe_w_gate, moe_w_up, moe_w_down):
    depth = w_in.shape[0]
    batch, seq, _ = x_prompt.shape
    dec_b, t_new, _ = x_sample.shape
    n_pages = page_table.shape[1]
    past = n_pages * PAGE_SIZE
    win_buf = state_win.shape[2]
    alpha = float((2 * depth) ** 0.25)
    assert seq % _KC == 0 and seq % GM_CHUNK == 0 and seq >= WINDOW
    assert t_new < CMP_STRIDE and t_new <= _NEW_PAD and past % SEL_BLOCK == 0 and win_buf == min(WINDOW, past)

    w_in_r, b_in_r = _prep_in_proj(w_in, b_in)
    pe8, wbd = _prep_cmp(cmp_pe, cmp_w)
    wc16, wn16, wg16, wo16 = (w.astype(BF16) for w in (w_up_conv, w_up_nsa, w_up_gm, w_o))
    mg16, mu16, md16 = moe_w_gate.astype(BF16), moe_w_up.astype(BF16), moe_w_down.astype(BF16)
    zr = jnp.zeros((depth, D_MODEL, _ROUTER_PAD - N_GROUPS - N_EXPERTS), F32)
    w_router = jnp.concatenate([router_g_w, router_e_w.reshape(depth, D_MODEL, N_EXPERTS), zr], axis=-1)
    zb = jnp.zeros((depth, _ROUTER_PAD - N_GROUPS - N_EXPERTS), F32)
    b_router = jnp.concatenate([router_g_b, router_e_b.reshape(depth, N_EXPERTS), zb], axis=-1)[:, None, :]
    gw = D_MODEL // GM_GROUPS
    wrow = jnp.repeat(gm_ws[:, :, :t_new, :t_new].transpose(0, 2, 3, 1).reshape(depth, t_new * t_new, GM_GROUPS), gw, axis=-1)
    bsrow = jnp.repeat(gm_bs[:, :, :t_new].transpose(0, 2, 1), gw, axis=-1)

    cache_t = cache_kv.transpose(0, 1, 3, 4, 5, 2).reshape(depth, cache_kv.shape[1], N_KV_SLOTS * KV_W, PAGE_SIZE)
    win_t_all = state_win.transpose(0, 1, 3, 4, 5, 2).reshape(depth, dec_b, 2 * KV_W, win_buf)

    kvc_s_all = _cmp_sample(page_table, cache_t, _deinterleave_perm(), pe8, wbd)

    xp = x_prompt.reshape(batch * seq, D_MODEL)
    xs = x_sample.reshape(dec_b * t_new, D_MODEL)
    kvp, kvs, wnp, wns, cvp, cvs, gmv = [], [], [], [], [], [], []
    wb = min(WINDOW, seq)
    eye_g = jnp.eye(N_KV, dtype=BF16)
    for l in range(depth):
        glu, q, kvf, kvh, u, v, gm, gn = _in_proj(xp, w_in_r[l], b_in_r[l])
        ca = _conv_prompt(glu, conv_w[l], conv_b[l], conv_ln_g[l], conv_ln_b[l], batch, seq)
        gc = _gmlp_prompt(u, v, gm_ln_g[l], gm_ln_b[l], gm_ws[l], gm_bs[l])
        kvc = _cmp_prompt(kvf, pe8[l], wbd[l], batch, seq)
        on = _nsa_prompt(q, gn, kvc, kvh, batch, seq)
        h = _merge(xp, ca, on, gc, gm, wc16[l], wn16[l], wg16[l], wo16[l], ln_g[l, 0], ln_b[l, 0], alpha)
        xp = _moe(h, w_router[l], b_router[l], mg16[l], mu16[l], md16[l], ln_g[l, 1], ln_b[l, 1], alpha)
        kv3 = kvf.reshape(batch, seq, KV_COLS)
        kvp.append(kv3[:, :, :N_KV_SLOTS * KV_W].reshape(batch, seq, N_KV_SLOTS, N_KV, HEAD_DIM))
        wnp.append(kv3[:, seq - wb:, N_KV_SLOTS * KV_W:].reshape(batch, wb, 2, N_KV, HEAD_DIM))
        cvp.append(glu.reshape(batch, seq, D_MODEL)[:, seq - (CONV_WIDTH - 1):])

        glu, q, kvf, kvh, u, v, gm, gn = _in_proj(xs, w_in_r[l], b_in_r[l])
        conv_ctx = jnp.concatenate([state_conv[l], glu.reshape(dec_b, t_new, D_MODEL)], axis=1)
        tmaj = lambda a: a.reshape(dec_b, t_new, D_MODEL).transpose(1, 0, 2)
        ca_t, gc_t, vn_t = _sample_seq(conv_ctx.transpose(1, 0, 2), tmaj(u), tmaj(v), conv_w[l], conv_b[l],
                                       conv_ln_g[l], conv_ln_b[l], gm_ln_g[l], gm_ln_b[l], wrow[l], bsrow[l])
        bmaj = lambda a: a.transpose(1, 0, 2).reshape(dec_b * t_new, D_MODEL)
        q5 = q.reshape(dec_b, t_new, N_KV, Q_PER_KV, HEAD_DIM).transpose(0, 2, 1, 3, 4)
        qbd = jnp.einsum('bgtrd,gk->bgtrkd', q5, eye_g).reshape(dec_b, N_KV * t_new * Q_PER_KV, KV_W)
        g5 = gn[:, :3 * N_HEADS].reshape(dec_b, t_new, N_KV, Q_PER_KV, 3).transpose(0, 2, 1, 3, 4)
        gates = jnp.pad(g5.reshape(dec_b, N_KV * t_new * Q_PER_KV, 3), ((0, 0), (0, 0), (0, LANES - 3)))
        new_rows = jnp.pad(kvh.reshape(dec_b, t_new, 4 * KV_W), ((0, 0), (0, _NEW_PAD - t_new), (0, 0)))
        o5 = _nsa_sample(page_table, qbd, gates, kvc_s_all[l], new_rows, win_t_all[l], cache_t, l, t_new)
        on = o5.reshape(dec_b, N_KV, t_new, Q_PER_KV, HEAD_DIM).transpose(0, 2, 1, 3, 4)
        on = on.reshape(dec_b * t_new, D_MODEL).astype(BF16)
        h = _merge(xs, bmaj(ca_t), on, bmaj(gc_t), gm, wc16[l], wn16[l], wg16[l], wo16[l], ln_g[l, 0], ln_b[l, 0], alpha)
        xs = _moe(h, w_router[l], b_router[l], mg16[l], mu16[l], md16[l], ln_g[l, 1], ln_b[l, 1], alpha)
        kv3 = kvf.reshape(dec_b, t_new, KV_COLS)
        kvs.append(kv3[:, :, :N_KV_SLOTS * KV_W].reshape(dec_b, t_new, N_KV_SLOTS, N_KV, HEAD_DIM))
        win_new = kv3[:, :, N_KV_SLOTS * KV_W:].reshape(dec_b, t_new, 2, N_KV, HEAD_DIM)
        wns.append(jnp.concatenate([state_win[l], win_new], axis=1)[:, t_new:])
        cvs.append(conv_ctx[:, t_new:])
        gmv.append(vn_t.transpose(1, 0, 2))

    return (xp.reshape(batch, seq, D_MODEL), xs.reshape(dec_b, t_new, D_MODEL), jnp.stack(kvp), jnp.stack(kvs),
            jnp.stack(wnp), jnp.stack(wns), jnp.stack(cvp), jnp.stack(cvs), jnp.stack(gmv))
```

```python
import functools

import jax
import jax.numpy as jnp
import numpy as np
from jax import lax
from jax.experimental import pallas as pl
from jax.experimental.pallas import tpu as pltpu

F32 = jnp.float32
BF16 = jnp.bfloat16

D_MODEL = 1024
CONV_WIDTH = 31
N_HEADS = 16
N_KV = 4
HEAD_DIM = 64
Q_PER_KV = N_HEADS // N_KV
CMP_BLOCK = 32
CMP_STRIDE = 16
CMP_SUB = CMP_BLOCK // CMP_STRIDE
SEL_BLOCK = 64
SEL_SHIFT = 6
SEL_TOPN = 8
WINDOW = 512
PAGE_SIZE = 128
N_KV_SLOTS = 4
FORCE_BONUS = 1.0e4
GM_GROUPS = 4
GM_CHUNK = 128
N_GROUPS = 4
EXPERTS_PER_GROUP = 4
EPG_SHIFT = 2
N_EXPERTS = N_GROUPS * EXPERTS_PER_GROUP
EXPERT_HIDDEN = 512
LN_EPS = 1e-5
NEG = -1e30
KV_W = N_KV * HEAD_DIM
KV_COLS = 6 * KV_W
GATE_PAD = 128

V7X_VMEM_BYTES = 64 * 1024 * 1024
VMEM_LIMIT = V7X_VMEM_BYTES * 7 // 8
LANES = 128
SUBLANES = 8

HIGHEST = lax.Precision.HIGHEST


def _cparams(sem):
    return pltpu.CompilerParams(dimension_semantics=sem, vmem_limit_bytes=VMEM_LIMIT)


def _pick(n, cands):
    for c in cands:
        if n % c == 0:
            return c
    raise ValueError(f"no tile in {cands} divides {n}")


def _sigmoid(x):
    return 1.0 / (1.0 + jnp.exp(-x))


def _gelu_tanh(x):
    return 0.5 * x * (1.0 + jnp.tanh(np.sqrt(2.0 / np.pi).astype(np.float32) * (x + 0.044715 * (x * x * x))))


def _layer_norm(x, g, b):
    mu = jnp.mean(x, axis=-1, keepdims=True)
    xc = x - mu
    var = jnp.mean(xc * xc, axis=-1, keepdims=True)
    return xc * lax.rsqrt(var + LN_EPS) * g + b


def _dot(a, b):
    return jnp.dot(a, b, preferred_element_type=F32)


def _dot_nt(a, b, precision=None):
    return lax.dot_general(a, b, (((1,), (1,)), ((), ())), preferred_element_type=F32, precision=precision)


def _dot_tn(a, b):
    return lax.dot_general(a, b, (((0,), (0,)), ((), ())), preferred_element_type=F32)


def _masked_softmax_rows(s, mask):
    s = jnp.where(mask, s, NEG)
    m = jnp.max(s, axis=-1, keepdims=True)
    e = jnp.exp(s - m)
    p = e / jnp.sum(e, axis=-1, keepdims=True)
    return jnp.where(mask, p, 0.0)


_C_A = 0
_C_GATE = _C_A + D_MODEL
_C_Q = _C_GATE + D_MODEL
_C_KV = _C_Q + D_MODEL
_C_U = _C_KV + KV_COLS
_C_V = _C_U + D_MODEL
_C_GM = _C_V + D_MODEL
_C_GN = _C_GM + 3 * D_MODEL
IN_COLS = _C_GN + GATE_PAD


def _in_proj_kernel(x_ref, w_ref, b_ref, wvt_ref, bvt_ref, glu_ref, q_ref, kv4_ref, kwn_ref, kvh_ref, u_ref, v_ref, gm_ref,
                    gn_ref, vt_ref):
    x = x_ref[...].astype(BF16)

    def seg(lo, hi):
        return _dot(x, w_ref[:, lo:hi]) + b_ref[:, lo:hi]

    glu_ref[...] = seg(_C_A, _C_GATE) * _sigmoid(seg(_C_GATE, _C_Q))
    q_ref[...] = (seg(_C_Q, _C_KV) * (HEAD_DIM ** -0.5)).astype(BF16)
    kv = seg(_C_KV, _C_U)
    kv4_ref[...] = kv[:, :N_KV_SLOTS * KV_W]
    kwn_ref[...] = kv[:, N_KV_SLOTS * KV_W:]
    kvh_ref[...] = kv[:, 2 * KV_W:].astype(BF16)
    u_ref[...] = _gelu_tanh(seg(_C_U, _C_V)).astype(BF16)
    v_ref[...] = _gelu_tanh(seg(_C_V, _C_GM))
    gm_ref[...] = _sigmoid(seg(_C_GM, _C_GN))
    gn_ref[...] = _sigmoid(seg(_C_GN, IN_COLS))
    vt_ref[...] = (_dot_nt(wvt_ref[...], x) + bvt_ref[:, 0:1]).astype(BF16)


def _in_proj(x, w_r, b_r, wvt, bvt):
    n = x.shape[0]
    tm = _pick(n, (256, 128, 64, 32, 16, 8))
    row = lambda w: pl.BlockSpec((tm, w), lambda i: (i, 0))
    return pl.pallas_call(
        _in_proj_kernel,
        grid=(n // tm,),
        in_specs=[row(D_MODEL),
                  pl.BlockSpec((D_MODEL, IN_COLS), lambda i: (0, 0), pipeline_mode=pl.Buffered(1)),
                  pl.BlockSpec((1, IN_COLS), lambda i: (0, 0)),
                  pl.BlockSpec((2 * KV_W, D_MODEL), lambda i: (0, 0)),
                  pl.BlockSpec((2 * KV_W, LANES), lambda i: (0, 0))],
        out_specs=[row(D_MODEL), row(D_MODEL), row(N_KV_SLOTS * KV_W), row(2 * KV_W), row(4 * KV_W), row(D_MODEL),
                   row(D_MODEL), row(3 * D_MODEL), row(GATE_PAD),
                   pl.BlockSpec((2 * KV_W, tm), lambda i: (0, i))],
        out_shape=[jax.ShapeDtypeStruct((n, D_MODEL), F32),
                   jax.ShapeDtypeStruct((n, D_MODEL), BF16),
                   jax.ShapeDtypeStruct((n, N_KV_SLOTS * KV_W), F32),
                   jax.ShapeDtypeStruct((n, 2 * KV_W), F32),
                   jax.ShapeDtypeStruct((n, 4 * KV_W), BF16),
                   jax.ShapeDtypeStruct((n, D_MODEL), BF16),
                   jax.ShapeDtypeStruct((n, D_MODEL), F32),
                   jax.ShapeDtypeStruct((n, 3 * D_MODEL), F32),
                   jax.ShapeDtypeStruct((n, GATE_PAD), F32),
                   jax.ShapeDtypeStruct((2 * KV_W, n), BF16)],
        compiler_params=_cparams(("parallel",)),
        name="in_proj",
    )(x, w_r, b_r, wvt, bvt)


_CONV_HALO = 32
_CONV_ROWS = SUBLANES
_CONV_LEAD = _CONV_HALO - (CONV_WIDTH - 1)
_CONV_SHIFT_EXTRA = 24


def _conv_prompt_kernel(halo_ref, glu_ref, w_ref, cb_ref, g_ref, b_ref, out_ref, win_ref, sh_ref, y_ref, *, tq):
    i = pl.program_id(1)
    win_ref[0:_CONV_HALO, :] = jnp.where(i > 0, halo_ref[...], 0.0)
    win_ref[_CONV_HALO:_CONV_HALO + tq, :] = glu_ref[...]
    n_sh = tq + _CONV_SHIFT_EXTRA
    for p in range(1, SUBLANES):
        sh_ref[p, 0:n_sh, :] = win_ref[p:p + n_sh, :]

    def chunk(c, carry):
        r0 = pl.multiple_of(c * _CONV_ROWS, _CONV_ROWS)
        acc = jnp.zeros((_CONV_ROWS, D_MODEL), F32)
        for k in range(CONV_WIDTH):
            phase, base = (_CONV_LEAD + k) % SUBLANES, (_CONV_LEAD + k) // SUBLANES * SUBLANES
            if phase == 0:
                x = win_ref[pl.ds(r0 + base, _CONV_ROWS), :]
            else:
                x = sh_ref[phase, pl.ds(r0 + base, _CONV_ROWS), :]
            acc = acc + x * w_ref[k]
        y_ref[pl.ds(r0, _CONV_ROWS), :] = acc
        return carry

    lax.fori_loop(0, tq // _CONV_ROWS, chunk, 0)
    y = _layer_norm(y_ref[...] + cb_ref[...], g_ref[...], b_ref[...])
    out_ref[...] = (y * _sigmoid(y)).astype(BF16)


def _conv_prompt(glu, conv_w, conv_b, ln_g, ln_b, batch, seq):
    tq = _pick(seq, (256, 128))
    glu3 = glu.reshape(batch, seq, D_MODEL)
    hb = tq // _CONV_HALO
    vec = pl.BlockSpec((1, D_MODEL), lambda b, i: (0, 0))
    out = pl.pallas_call(
        functools.partial(_conv_prompt_kernel, tq=tq),
        grid=(batch, seq // tq),
        in_specs=[pl.BlockSpec((None, _CONV_HALO, D_MODEL), lambda b, i: (b, jnp.maximum(i * hb - 1, 0), 0)),
                  pl.BlockSpec((None, tq, D_MODEL), lambda b, i: (b, i, 0)),
                  pl.BlockSpec((CONV_WIDTH, SUBLANES, D_MODEL), lambda b, i: (0, 0, 0)),
                  vec, vec, vec],
        out_specs=pl.BlockSpec((None, tq, D_MODEL), lambda b, i: (b, i, 0)),
        out_shape=jax.ShapeDtypeStruct((batch, seq, D_MODEL), BF16),
        scratch_shapes=[pltpu.VMEM((_CONV_HALO + tq, D_MODEL), F32),
                        pltpu.VMEM((SUBLANES, tq + _CONV_SHIFT_EXTRA, D_MODEL), F32),
                        pltpu.VMEM((tq, D_MODEL), F32)],
        compiler_params=_cparams(("parallel", "parallel")),
        name="conv_prompt",
    )(glu3, glu3, jnp.broadcast_to(conv_w[:, None, :], (CONV_WIDTH, SUBLANES, D_MODEL)), conv_b[None], ln_g[None],
      ln_b[None])
    return out.reshape(batch * seq, D_MODEL)


def _gmlp_prompt_kernel(u_ref, v_ref, g_ref, b_ref, ws_ref, bs_ref, out_ref):
    vn = _layer_norm(v_ref[...], g_ref[...], b_ref[...]).astype(BF16)
    ri = lax.broadcasted_iota(jnp.int32, (GM_CHUNK, GM_CHUNK), 0)
    ci = lax.broadcasted_iota(jnp.int32, (GM_CHUNK, GM_CHUNK), 1)
    gw = D_MODEL // GM_GROUPS
    for g in range(GM_GROUPS):
        w = jnp.where(ri >= ci, ws_ref[g], 0.0).astype(BF16)
        mixed = _dot(w, vn[:, g * gw:(g + 1) * gw]) + bs_ref[:, g:g + 1]
        out_ref[:, g * gw:(g + 1) * gw] = (u_ref[:, g * gw:(g + 1) * gw].astype(F32) * mixed).astype(BF16)


def _gmlp_prompt(u, v, ln_g, ln_b, gm_ws, gm_bs):
    n = u.shape[0]
    vec = pl.BlockSpec((1, D_MODEL), lambda i: (0, 0))
    row = pl.BlockSpec((GM_CHUNK, D_MODEL), lambda i: (i, 0))
    return pl.pallas_call(
        _gmlp_prompt_kernel,
        grid=(n // GM_CHUNK,),
        in_specs=[row, row, vec, vec,
                  pl.BlockSpec((GM_GROUPS, GM_CHUNK, GM_CHUNK), lambda i: (0, 0, 0)),
                  pl.BlockSpec((GM_CHUNK, GM_GROUPS), lambda i: (0, 0))],
        out_specs=row,
        out_shape=jax.ShapeDtypeStruct((n, D_MODEL), BF16),
        compiler_params=_cparams(("parallel",)),
        name="gmlp_prompt",
    )(u, v, ln_g[None], ln_b[None], gm_ws, gm_bs.T)


def _sample_seq_kernel(ctx_ref, u_ref, v_ref, cw_ref, cb_ref, cg_ref, cbb_ref, gg_ref, gb_ref, wrow_ref, bsrow_ref,
                       ca_ref, gc_ref, vn_ref, *, t_new):
    for t in range(t_new):
        acc = ctx_ref[t] * cw_ref[0:1, :]
        for k in range(1, CONV_WIDTH):
            acc = acc + ctx_ref[t + k] * cw_ref[k:k + 1, :]
        y = _layer_norm(acc + cb_ref[...], cg_ref[...], cbb_ref[...])
        ca_ref[t] = (y * _sigmoid(y)).astype(BF16)
    for t in range(t_new):
        vn_ref[t] = _layer_norm(v_ref[t], gg_ref[...], gb_ref[...])
    for i in range(t_new):
        mixed = bsrow_ref[i:i + 1, :]
        for j in range(i + 1):
            mixed = mixed + wrow_ref[i * t_new + j:i * t_new + j + 1, :] * vn_ref[j]
        gc_ref[i] = (u_ref[i].astype(F32) * mixed).astype(BF16)


def _sample_seq(ctx_t, u_t, v_t, conv_w, conv_b, cg, cb, gg, gb, wrow, bsrow):
    rows, dec_b, _ = ctx_t.shape
    t_new = u_t.shape[0]
    sb = _pick(dec_b, (32, 16, 8))
    vec = pl.BlockSpec((1, D_MODEL), lambda i: (0, 0))
    blk = lambda r: pl.BlockSpec((r, sb, D_MODEL), lambda i: (0, i, 0))
    full = lambda r: pl.BlockSpec((r, D_MODEL), lambda i: (0, 0))
    return pl.pallas_call(
        functools.partial(_sample_seq_kernel, t_new=t_new),
        grid=(dec_b // sb,),
        in_specs=[blk(rows), blk(t_new), blk(t_new), full(CONV_WIDTH), vec, vec, vec, vec, vec,
                  full(t_new * t_new), full(t_new)],
        out_specs=[blk(t_new), blk(t_new), blk(t_new)],
        out_shape=[jax.ShapeDtypeStruct((t_new, dec_b, D_MODEL), BF16),
                   jax.ShapeDtypeStruct((t_new, dec_b, D_MODEL), BF16),
                   jax.ShapeDtypeStruct((t_new, dec_b, D_MODEL), F32)],
        compiler_params=_cparams(("parallel",)),
        name="sample_seq",
    )(ctx_t, u_t, v_t, conv_w, conv_b[None], cg[None], cb[None], gg[None], gb[None], wrow, bsrow)


_CMP_K = CMP_STRIDE * KV_W
_CMP_N = CMP_SUB * KV_W


def _combine_halves(y, bias):
    first = y[:, :KV_W] + bias[0:1, :KV_W]
    second = y[:, KV_W:] + bias[1:2, KV_W:]
    return first + pltpu.roll(second, shift=y.shape[0] - 1, axis=0)


def _cmp_prompt_kernel(lo_ref, hi_ref, pe_ref, w_ref, out_ref, *, n_chunks):
    bias = _dot(pe_ref[...].astype(BF16), w_ref[...])
    y = jnp.zeros((n_chunks, _CMP_N), F32)
    for l in range(CMP_STRIDE):
        rows = jnp.concatenate([lo_ref[pl.ds(l, n_chunks, stride=CMP_STRIDE), :],
                                hi_ref[pl.ds(l, n_chunks, stride=CMP_STRIDE), :]], axis=-1).astype(BF16)
        y = y + _dot(rows, w_ref[l * KV_W:(l + 1) * KV_W, :])
    out_ref[...] = _combine_halves(y, bias)


def _cmp_prompt(kv4, pe8, wbd, batch, seq):
    n_chunks = seq // CMP_STRIDE
    kv3 = kv4.reshape(batch, seq, N_KV_SLOTS * KV_W)
    return pl.pallas_call(
        functools.partial(_cmp_prompt_kernel, n_chunks=n_chunks),
        grid=(batch, 2),
        in_specs=[pl.BlockSpec((None, seq, LANES), lambda b, s: (b, 0, 2 * s)),
                  pl.BlockSpec((None, seq, LANES), lambda b, s: (b, 0, 2 * s + 1)),
                  pl.BlockSpec((None, SUBLANES, _CMP_K), lambda b, s: (s, 0, 0)),
                  pl.BlockSpec((None, _CMP_K, _CMP_N), lambda b, s: (s, 0, 0))],
        out_specs=pl.BlockSpec((None, None, n_chunks, KV_W), lambda b, s: (b, s, 0, 0)),
        out_shape=jax.ShapeDtypeStruct((batch, 2, n_chunks, KV_W), F32),
        compiler_params=_cparams(("parallel", "parallel")),
        name="cmp_prompt",
    )(kv3, kv3, pe8, wbd)


def _cmp_page_copy(cache_ref, buf_ref, sem_ref, pt_ref, layer, sample, page, slot, k, n_pages):
    pid = pt_ref[sample * n_pages + page]
    return pltpu.make_async_copy(cache_ref.at[layer, pid, pl.ds(0, 2 * KV_W), :], buf_ref.at[slot, k], sem_ref.at[slot])


def _cmp_sample_kernel(pt_ref, perm_ref, pe_ref, w_ref, cache_ref, out_ref, buf_ref, rk_ref, rv_ref, sem_ref,
                       *, sb, n_pages, steps_per_layer, n_steps):
    step = pl.program_id(0)
    slot = step % 2
    per_step = sb * n_pages

    def issue(st, sl):
        layer = st // steps_per_layer
        s0 = (st % steps_per_layer) * sb

        def body(k, carry):
            _cmp_page_copy(cache_ref, buf_ref, sem_ref, pt_ref, layer, s0 + k // n_pages, k % n_pages, sl, k,
                           n_pages).start()
            return carry

        lax.fori_loop(0, per_step, body, 0)

    @pl.when(step == 0)
    def _():
        issue(step, slot)

    @pl.when(step + 1 < n_steps)
    def _():
        issue(step + 1, 1 - slot)

    def wait_body(k, carry):
        _cmp_page_copy(cache_ref, buf_ref, sem_ref, pt_ref, 0, 0, 0, slot, k, n_pages).wait()
        return carry

    lax.fori_loop(0, per_step, wait_body, 0)

    chunks_per_page = PAGE_SIZE // CMP_STRIDE

    def page_body(k, carry):
        a = buf_ref[slot, k].astype(BF16)
        t = _dot_nt(perm_ref[...], a)
        r0 = pl.multiple_of(k * chunks_per_page, chunks_per_page)
        for l in range(CMP_STRIDE):
            piece = t[l * chunks_per_page:(l + 1) * chunks_per_page, :]
            rk_ref[pl.ds(r0, chunks_per_page), l * KV_W:(l + 1) * KV_W] = piece[:, :KV_W]
            rv_ref[pl.ds(r0, chunks_per_page), l * KV_W:(l + 1) * KV_W] = piece[:, KV_W:]
        return carry

    lax.fori_loop(0, per_step, page_body, 0, unroll=_pick(per_step, (4, 2, 1)))

    n_chunks = n_pages * chunks_per_page
    for s, r_ref in ((0, rk_ref), (1, rv_ref)):
        bias = _dot(pe_ref[s].astype(BF16), w_ref[s])
        y = _dot(r_ref[...].astype(BF16), w_ref[s])
        blocks = _combine_halves(y, bias)
        for j in range(sb):
            out_ref[j, s] = blocks[j * n_chunks:(j + 1) * n_chunks]


def _cmp_sample(page_table, cache_t, perm, pe8, wbd):
    depth = cache_t.shape[0]
    dec_b, n_pages = page_table.shape
    sb = _pick(dec_b, (2, 1))
    steps_per_layer = dec_b // sb
    n_steps = depth * steps_per_layer
    n_chunks = n_pages * (PAGE_SIZE // CMP_STRIDE)
    rows = sb * n_chunks
    grid_spec = pltpu.PrefetchScalarGridSpec(
        num_scalar_prefetch=1,
        grid=(n_steps,),
        in_specs=[pl.BlockSpec((PAGE_SIZE, PAGE_SIZE), lambda i, pt: (0, 0)),
                  pl.BlockSpec((None, 2, SUBLANES, _CMP_K), lambda i, pt: (i // steps_per_layer, 0, 0, 0)),
                  pl.BlockSpec((None, 2, _CMP_K, _CMP_N), lambda i, pt: (i // steps_per_layer, 0, 0, 0)),
                  pl.BlockSpec(memory_space=pl.ANY)],
        out_specs=pl.BlockSpec((sb, 2, n_chunks, KV_W), lambda i, pt: (i, 0, 0, 0)),
        scratch_shapes=[pltpu.VMEM((2, sb * n_pages, 2 * KV_W, PAGE_SIZE), F32),
                        pltpu.VMEM((rows, _CMP_K), F32),
                        pltpu.VMEM((rows, _CMP_K), F32),
                        pltpu.SemaphoreType.DMA((2,))])
    out = pl.pallas_call(
        functools.partial(_cmp_sample_kernel, sb=sb, n_pages=n_pages, steps_per_layer=steps_per_layer,
                          n_steps=n_steps),
        grid_spec=grid_spec,
        out_shape=jax.ShapeDtypeStruct((depth * dec_b, 2, n_chunks, KV_W), F32),
        compiler_params=_cparams(("arbitrary",)),
        name="cmp_sample",
    )(page_table.reshape(-1), perm, pe8, wbd, cache_t)
    return out.reshape(depth, dec_b, 2, n_chunks, KV_W)


_TQ = 256
_KC = 256


def _rank_select(score, blk, n_blocks, axis):
    rank = jnp.zeros(score.shape, F32)
    for sp in range(n_blocks):
        row = lax.slice_in_dim(score, sp, sp + 1, axis=axis)
        before = jnp.where(blk > sp, 1.0, 0.0)
        rank = rank + jnp.where(row > score, 1.0, jnp.where(row == score, before, 0.0))
    return jnp.where(rank < float(min(SEL_TOPN, n_blocks)), 1.0, 0.0)


def _nsa_prompt_kernel(q_ref, gnt_ref, kvc_ref, kh_ref, vt_ref, mt_ref, o_ref, bias_ref, *, seq, n_cmp, n_sel, n_sel_pad):
    i = pl.program_id(1)
    t0 = i * _TQ
    cols = Q_PER_KV * _TQ
    n_chunks = kvc_ref.shape[1]
    win_keys = min(WINDOW + _TQ, seq)
    bpc = _KC // SEL_BLOCK

    pos = t0 + (lax.broadcasted_iota(jnp.int32, (1, cols), 1) & (_TQ - 1))
    pos_t = t0 + lax.broadcasted_iota(jnp.int32, (1, _TQ), 1)

    n_idx = lax.broadcasted_iota(jnp.int32, (n_chunks, 1), 0)
    mask_c = (n_idx * CMP_STRIDE + (CMP_BLOCK - 1) <= pos) & (n_idx < n_cmp)

    blk = lax.broadcasted_iota(jnp.int32, (n_sel_pad, _TQ), 0)
    cur = jnp.right_shift(pos_t, SEL_SHIFT)
    valid = (blk <= cur) & (blk < n_sel)
    forced = (blk == 0) | (blk == cur) | (blk == cur - 1)

    w_start = pl.multiple_of(jnp.maximum(t0 + _TQ - win_keys, 0), _TQ)
    wpos = w_start + lax.broadcasted_iota(jnp.int32, (win_keys, 1), 0)
    bias_w = jnp.where((wpos <= pos) & (wpos > pos - WINDOW), 0.0, NEG)
    n_kc = (t0 + _TQ + _KC - 1) // _KC
    kidx = lax.broadcasted_iota(jnp.int32, (_KC, 1), 0)

    qgs, o_cs = [], []
    for g in range(N_KV):
        lo = g * HEAD_DIM
        qg = jnp.concatenate([q_ref[:, (g * Q_PER_KV + r) * HEAD_DIM:(g * Q_PER_KV + r + 1) * HEAD_DIM]
                              for r in range(Q_PER_KV)], axis=0)
        qgs.append(qg)

        kc = kvc_ref[0, :, lo:lo + HEAD_DIM].astype(BF16)
        vc = kvc_ref[1, :, lo:lo + HEAD_DIM].astype(BF16)
        s_c = jnp.where(mask_c, _dot_nt(kc, qg), NEG)
        e_c = jnp.exp(s_c - jnp.max(s_c, axis=0, keepdims=True))
        p_c = jnp.where(mask_c, e_c / jnp.sum(e_c, axis=0, keepdims=True), 0.0)
        o_cs.append(_dot_tn(vc, p_c.astype(BF16)))

        psum = p_c[:, 0:_TQ]
        for r in range(1, Q_PER_KV):
            psum = psum + p_c[:, r * _TQ:(r + 1) * _TQ]
        imp = jnp.dot(mt_ref[...], psum, precision=HIGHEST, preferred_element_type=F32)
        score = jnp.where(valid, imp + jnp.where(forced, FORCE_BONUS, 0.0), -FORCE_BONUS)
        score = jnp.where(blk < n_sel, score, -jnp.inf)
        sel = _rank_select(score, blk, n_sel, axis=0)
        bias_sel = jnp.where(sel > 0.5, 0.0, NEG)
        for c in range(seq // _KC):
            bias_ref[g, c, 0:bpc, :] = bias_sel[c * bpc:(c + 1) * bpc, :]

    def sel_chunk(c, carry, causal):
        k0 = pl.multiple_of(c * _KC, _KC)
        out = []
        for g in range(N_KV):
            lo = g * HEAD_DIM
            m, l, acc = carry[g]
            ks = kh_ref[pl.ds(k0, _KC), lo:lo + HEAD_DIM]
            vst = vt_ref[lo:lo + HEAD_DIM, pl.ds(k0, _KC)]
            bb = bias_ref[g, c]
            bias = jnp.concatenate([jnp.broadcast_to(bb[j:j + 1, :], (SEL_BLOCK, _TQ)) for j in range(bpc)], axis=0)
            s = _dot_nt(ks, qgs[g]) + jnp.concatenate([bias] * Q_PER_KV, axis=1)
            if causal:
                s = jnp.where(k0 + kidx <= pos, s, NEG)
            m_new = jnp.maximum(m, jnp.max(s, axis=0, keepdims=True))
            alpha = jnp.exp(m - m_new)
            p = jnp.exp(s - m_new)
            l = alpha * l + jnp.sum(p, axis=0, keepdims=True)
            acc = alpha * acc + _dot(vst, p.astype(BF16))
            out.append((m_new, l, acc))
        return tuple(out)

    carry = tuple((jnp.full((1, cols), NEG, F32), jnp.zeros((1, cols), F32), jnp.zeros((HEAD_DIM, cols), F32))
                  for _ in range(N_KV))
    carry = lax.fori_loop(0, n_kc - 1, functools.partial(sel_chunk, causal=False), carry)
    carry = sel_chunk(n_kc - 1, carry, causal=True)

    for g in range(N_KV):
        lo = g * HEAD_DIM
        qg, o_c = qgs[g], o_cs[g]
        _, l_s, acc_s = carry[g]
        o_s = acc_s * jnp.where(l_s > 0.0, 1.0 / l_s, 0.0)

        kw = kh_ref[pl.ds(w_start, win_keys), 2 * KV_W + lo:2 * KV_W + lo + HEAD_DIM]
        vwt = vt_ref[KV_W + lo:KV_W + lo + HEAD_DIM, pl.ds(w_start, win_keys)]
        s_w = _dot_nt(kw, qg) + bias_w
        e_w = jnp.exp(s_w - jnp.max(s_w, axis=0, keepdims=True))
        o_w = _dot(vwt, e_w.astype(BF16)) * (1.0 / jnp.sum(e_w, axis=0, keepdims=True))

        for r in range(Q_PER_KV):
            h = g * Q_PER_KV + r
            sl = slice(r * _TQ, (r + 1) * _TQ)
            o = (gnt_ref[3 * h:3 * h + 1, :] * o_c[:, sl] + gnt_ref[3 * h + 1:3 * h + 2, :] * o_s[:, sl]
                 + gnt_ref[3 * h + 2:3 * h + 3, :] * o_w[:, sl])
            o_ref[h * HEAD_DIM:(h + 1) * HEAD_DIM, :] = o.astype(BF16)


def _nsa_prompt(q, gn, kvc, kvh, vt, batch, seq):
    n_chunks = seq // CMP_STRIDE
    n_cmp = (seq - CMP_BLOCK) // CMP_STRIDE + 1
    n_sel = seq // SEL_BLOCK
    n_sel_pad = -(-n_sel // SUBLANES) * SUBLANES
    mt = _cmp_to_sel_t(n_chunks, n_cmp, n_sel, n_sel_pad)
    gnt = gn.reshape(batch, seq, GATE_PAD).transpose(0, 2, 1)
    out = pl.pallas_call(
        functools.partial(_nsa_prompt_kernel, seq=seq, n_cmp=n_cmp, n_sel=n_sel, n_sel_pad=n_sel_pad),
        grid=(batch, seq // _TQ),
        in_specs=[pl.BlockSpec((None, _TQ, D_MODEL), lambda b, i: (b, i, 0)),
                  pl.BlockSpec((None, GATE_PAD, _TQ), lambda b, i: (b, 0, i)),
                  pl.BlockSpec((None, 2, n_chunks, KV_W), lambda b, i: (b, 0, 0, 0)),
                  pl.BlockSpec((None, seq, 4 * KV_W), lambda b, i: (b, 0, 0)),
                  pl.BlockSpec((2 * KV_W, seq), lambda b, i: (0, b)),
                  pl.BlockSpec((n_sel_pad, n_chunks), lambda b, i: (0, 0))],
        out_specs=pl.BlockSpec((None, D_MODEL, _TQ), lambda b, i: (b, 0, i)),
        out_shape=jax.ShapeDtypeStruct((batch, D_MODEL, seq), BF16),
        scratch_shapes=[pltpu.VMEM((N_KV, seq // _KC, SUBLANES, _TQ), F32)],
        compiler_params=_cparams(("parallel", "arbitrary")),
        name="nsa_prompt",
    )(q.reshape(batch, seq, D_MODEL), gnt, kvc, kvh.reshape(batch, seq, 4 * KV_W), vt, mt)
    return out.transpose(0, 2, 1).reshape(batch * seq, D_MODEL)


def _cmp_to_sel_t(n_chunks, n_cmp, n_sel, n_sel_pad):
    cs = np.arange(n_chunks)[None, :] * CMP_STRIDE
    ss = np.arange(n_sel_pad)[:, None] * SEL_BLOCK
    m = (cs < ss + SEL_BLOCK) & (cs + CMP_BLOCK > ss)
    m = m & (np.arange(n_chunks)[None, :] < n_cmp) & (np.arange(n_sel_pad)[:, None] < n_sel)
    return jnp.asarray(m, F32)


def _block_expand(rows, keys):
    return jnp.asarray(np.arange(keys)[None, :] // SEL_BLOCK == np.arange(rows)[:, None], BF16)


_NEW_PAD = 128


def _slc_page_copy(cache_ref, buf_ref, sem_ref, pt_ref, layer, sample, page, slot, n_pages):
    pid = pt_ref[sample * n_pages + page]
    return pltpu.make_async_copy(cache_ref.at[layer, pid, pl.ds(2 * KV_W, 2 * KV_W), :],
                                 buf_ref.at[slot, :, pl.ds(page * PAGE_SIZE, PAGE_SIZE)], sem_ref.at[slot])


def _joint_softmax(s_a, m_a, s_b, m_b):
    mx = jnp.maximum(jnp.max(jnp.where(m_a, s_a, NEG), axis=-1, keepdims=True),
                     jnp.max(jnp.where(m_b, s_b, NEG), axis=-1, keepdims=True))
    e_a = jnp.where(m_a, jnp.exp(s_a - mx), 0.0)
    e_b = jnp.where(m_b, jnp.exp(s_b - mx), 0.0)
    den = jnp.sum(e_a, axis=-1, keepdims=True) + jnp.sum(e_b, axis=-1, keepdims=True)
    return e_a, e_b, jnp.where(den > 0.0, 1.0 / den, 0.0)


def _nsa_sample_kernel(pt_ref, qbd_ref, gate_ref, kvc_ref, new_ref, win_ref, mt_ref, ex_ref, rr_ref, meta_ref,
                       cache_ref, o_ref, buf_ref, sem_ref, *, layer, n_pages, t_new, n_cmp, n_sel, dec_b):
    smp = pl.program_id(0)
    slot = smp % 2
    past = n_pages * PAGE_SIZE
    rows = N_KV * t_new * Q_PER_KV
    win_rows = win_ref.shape[1]

    def issue(s, sl):
        for page in range(n_pages):
            _slc_page_copy(cache_ref, buf_ref, sem_ref, pt_ref, layer, s, page, sl, n_pages).start()

    @pl.when(smp == 0)
    def _():
        issue(smp, slot)

    @pl.when(smp + 1 < dec_b)
    def _():
        issue(smp + 1, 1 - slot)

    for page in range(n_pages):
        _slc_page_copy(cache_ref, buf_ref, sem_ref, pt_ref, layer, 0, page, slot, n_pages).wait()

    qbd = qbd_ref[...]
    tok = meta_ref[:, 0:1]
    grp = meta_ref[:, 1:2]
    pos = past + tok

    n_chunks = kvc_ref.shape[1]
    n_idx = lax.broadcasted_iota(jnp.int32, (1, n_chunks), 1)
    mask_c = (n_idx * CMP_STRIDE + (CMP_BLOCK - 1) <= pos) & (n_idx < n_cmp)
    p_c = _masked_softmax_rows(_dot_nt(qbd, kvc_ref[0].astype(BF16)), mask_c)
    o_c = _dot(p_c.astype(BF16), kvc_ref[1].astype(BF16))

    imp = jnp.dot(rr_ref[...], jnp.dot(p_c, mt_ref[...], precision=HIGHEST, preferred_element_type=F32),
                  precision=HIGHEST, preferred_element_type=F32)
    blk = lax.broadcasted_iota(jnp.int32, (rows, LANES), 1)
    cur = jnp.right_shift(pos, SEL_SHIFT)
    valid = (blk <= cur) & (blk < n_sel)
    forced = (blk == 0) | (blk == cur) | (blk == cur - 1)
    score = jnp.where(valid, imp + jnp.where(forced, FORCE_BONUS, 0.0), -FORCE_BONUS)
    score = jnp.where(blk < n_sel, score, -jnp.inf)
    sel = _rank_select(score, blk, n_sel, axis=1)
    mask_s = _dot(sel.astype(BF16), ex_ref[...]) > 0.5

    i_new = lax.broadcasted_iota(jnp.int32, (1, _NEW_PAD), 1)
    new_ok = (i_new < t_new) & (i_new <= tok)
    new_blk = past // SEL_BLOCK
    mask_sn = new_ok & (sel[:, new_blk:new_blk + 1] > 0.5)

    ks = buf_ref[slot, 0:KV_W, :].astype(BF16)
    vs = buf_ref[slot, KV_W:2 * KV_W, :].astype(BF16)
    s_s = _dot(qbd, ks)
    s_sn = _dot_nt(qbd, new_ref[:, 0:KV_W])
    e_s, e_sn, inv_s = _joint_softmax(s_s, mask_s, s_sn, mask_sn)
    o_s = (_dot_nt(e_s.astype(BF16), vs) + _dot(e_sn.astype(BF16), new_ref[:, KV_W:2 * KV_W])) * inv_s

    kw = win_ref[0:KV_W, :].astype(BF16)
    vw = win_ref[KV_W:2 * KV_W, :].astype(BF16)
    i_w = lax.broadcasted_iota(jnp.int32, (1, win_rows), 1)
    mask_w = i_w > tok + (win_rows - WINDOW)
    s_w = _dot(qbd, kw)
    s_wn = _dot_nt(qbd, new_ref[:, 2 * KV_W:3 * KV_W])
    e_w, e_wn, inv_w = _joint_softmax(s_w, mask_w, s_wn, new_ok)
    o_w = (_dot_nt(e_w.astype(BF16), vw) + _dot(e_wn.astype(BF16), new_ref[:, 3 * KV_W:4 * KV_W])) * inv_w

    o_all = gate_ref[:, 0:1] * o_c + gate_ref[:, 1:2] * o_s + gate_ref[:, 2:3] * o_w
    out = jnp.zeros((rows, HEAD_DIM), F32)
    for g in range(N_KV):
        out = out + jnp.where(grp == g, o_all[:, g * HEAD_DIM:(g + 1) * HEAD_DIM], 0.0)
    o_ref[...] = out


def _nsa_sample(page_table, qbd, gates, kvc, new_rows, win_t, cache_t, layer, t_new):
    dec_b, n_pages = page_table.shape
    past = n_pages * PAGE_SIZE
    rows = N_KV * t_new * Q_PER_KV
    n_chunks = kvc.shape[3]
    n_cmp = (past + t_new - CMP_BLOCK) // CMP_STRIDE + 1
    n_sel = -(-(past + t_new) // SEL_BLOCK)
    win_rows = win_t.shape[-1]
    mt = _cmp_to_sel_t(n_chunks, n_cmp, n_sel, LANES).T
    ex = _block_expand(LANES, past)
    rr = jnp.asarray(np.arange(rows)[:, None] // Q_PER_KV == np.arange(rows)[None, :] // Q_PER_KV, F32)
    meta_np = np.zeros((rows, LANES), np.int32)
    meta_np[:, 0] = (np.arange(rows) // Q_PER_KV) % t_new
    meta_np[:, 1] = np.arange(rows) // (t_new * Q_PER_KV)
    meta = jnp.asarray(meta_np)
    per = lambda *shape: pl.BlockSpec((None,) + shape, lambda s, pt: (s,) + (0,) * len(shape))
    const = lambda *shape: pl.BlockSpec(shape, lambda s, pt: (0,) * len(shape))
    lper = lambda *shape: pl.BlockSpec((None, None) + shape, lambda s, pt: (layer, s) + (0,) * len(shape))
    grid_spec = pltpu.PrefetchScalarGridSpec(
        num_scalar_prefetch=1,
        grid=(dec_b,),
        in_specs=[per(rows, KV_W), per(rows, LANES), lper(2, n_chunks, KV_W), per(_NEW_PAD, 4 * KV_W),
                  lper(2 * KV_W, win_rows), const(n_chunks, LANES), const(LANES, past), const(rows, rows),
                  const(rows, LANES), pl.BlockSpec(memory_space=pl.ANY)],
        out_specs=per(rows, HEAD_DIM),
        scratch_shapes=[pltpu.VMEM((2, 2 * KV_W, past), F32), pltpu.SemaphoreType.DMA((2,))])
    return pl.pallas_call(
        functools.partial(_nsa_sample_kernel, layer=layer, n_pages=n_pages, t_new=t_new, n_cmp=n_cmp, n_sel=n_sel,
                          dec_b=dec_b),
        grid_spec=grid_spec,
        out_shape=jax.ShapeDtypeStruct((dec_b, rows, HEAD_DIM), F32),
        compiler_params=_cparams(("arbitrary",)),
        name="nsa_sample",
    )(page_table.reshape(-1), qbd, gates, kvc, new_rows, win_t, mt, ex, rr, meta, cache_t)


def _merge_kernel(x_ref, ca_ref, on_ref, gc_ref, gm_ref, wc_ref, wn_ref, wg_ref, wo_ref, g_ref, b_ref, h_ref, *, alpha):
    ya = _dot(ca_ref[...], wc_ref[...])
    yb = _dot(on_ref[...], wn_ref[...])
    yc = _dot(gc_ref[...], wg_ref[...])
    m = (gm_ref[:, 0:D_MODEL] * ya + gm_ref[:, D_MODEL:2 * D_MODEL] * yb + gm_ref[:, 2 * D_MODEL:3 * D_MODEL] * yc)
    mix = _dot(m.astype(BF16), wo_ref[...])
    h_ref[...] = _layer_norm(alpha * x_ref[...] + mix, g_ref[...], b_ref[...])


def _merge(x, ca, on, gc, gm, wc, wn, wg, wo, ln_g, ln_b, alpha):
    n = x.shape[0]
    tm = _pick(n, (256, 128, 64, 32, 16, 8))
    row = lambda w: pl.BlockSpec((tm, w), lambda i: (i, 0))
    wspec = pl.BlockSpec((D_MODEL, D_MODEL), lambda i: (0, 0), pipeline_mode=pl.Buffered(1))
    vec = pl.BlockSpec((1, D_MODEL), lambda i: (0, 0))
    return pl.pallas_call(
        functools.partial(_merge_kernel, alpha=alpha),
        grid=(n // tm,),
        in_specs=[row(D_MODEL), row(D_MODEL), row(D_MODEL), row(D_MODEL), row(3 * D_MODEL),
                  wspec, wspec, wspec, wspec, vec, vec],
        out_specs=row(D_MODEL),
        out_shape=jax.ShapeDtypeStruct((n, D_MODEL), F32),
        compiler_params=_cparams(("parallel",)),
        name="merge",
    )(x, ca, on, gc, gm, wc, wn, wg, wo, ln_g[None], ln_b[None])


_ROUTER_PAD = 128


def _moe_kernel(h_ref, wr_ref, br_ref, wg_ref, wu_ref, wd_ref, g_ref, b_ref, o_ref, hb_ref, wt_ref, acc_ref, *, alpha):
    e = pl.program_id(1)
    lane = lax.broadcasted_iota(jnp.int32, (1, _ROUTER_PAD), 1)

    @pl.when(e == 0)
    def _():
        h = h_ref[...]
        hb_ref[...] = h.astype(BF16)
        logit = jnp.dot(h, wr_ref[...], precision=HIGHEST, preferred_element_type=F32) + br_ref[...]
        gl = jnp.where(lane < N_GROUPS, logit, -jnp.inf)
        gmax = jnp.max(gl, axis=-1, keepdims=True)
        g_sel = jnp.min(jnp.where(gl == gmax, lane, _ROUTER_PAD), axis=-1, keepdims=True)
        p_grp = 1.0 / jnp.sum(jnp.exp(gl - gmax), axis=-1, keepdims=True)
        lane_grp = jnp.right_shift(lane - N_GROUPS, EPG_SHIFT)
        in_grp = (lane >= N_GROUPS) & (lane < N_GROUPS + N_EXPERTS) & (lane_grp == g_sel)
        el = jnp.where(in_grp, logit, -jnp.inf)
        v1 = jnp.max(el, axis=-1, keepdims=True)
        i1 = jnp.min(jnp.where(el == v1, lane, _ROUTER_PAD), axis=-1, keepdims=True)
        el2 = jnp.where(lane == i1, -jnp.inf, el)
        v2 = jnp.max(el2, axis=-1, keepdims=True)
        i2 = jnp.min(jnp.where(el2 == v2, lane, _ROUTER_PAD), axis=-1, keepdims=True)
        e2 = jnp.exp(v2 - v1)
        p1 = 1.0 / (1.0 + e2)
        p2 = e2 / (1.0 + e2)
        wt_ref[...] = jnp.where(lane == i1, p1, jnp.where(lane == i2, p2, 0.0)) * p_grp
        acc_ref[...] = jnp.zeros_like(acc_ref)

    wt = jnp.sum(jnp.where(lane == N_GROUPS + e, wt_ref[...], 0.0), axis=-1, keepdims=True)
    hb = hb_ref[...]
    a = _dot(hb, wg_ref[...])
    b = _dot(hb, wu_ref[...])
    hid = (a * _sigmoid(a)) * b * wt
    acc_ref[...] += _dot(hid.astype(BF16), wd_ref[...])

    @pl.when(e == N_EXPERTS - 1)
    def _():
        o_ref[...] = _layer_norm(alpha * h_ref[...] + acc_ref[...], g_ref[...], b_ref[...])


def _moe(h, wr, br, wg, wu, wd, ln_g, ln_b, alpha):
    n = h.shape[0]
    tm = _pick(n, (512, 256, 128, 64, 32, 16, 8))
    row = pl.BlockSpec((tm, D_MODEL), lambda i, e: (i, 0))
    vec = pl.BlockSpec((1, D_MODEL), lambda i, e: (0, 0))
    return pl.pallas_call(
        functools.partial(_moe_kernel, alpha=alpha),
        grid=(n // tm, N_EXPERTS),
        in_specs=[row,
                  pl.BlockSpec((D_MODEL, _ROUTER_PAD), lambda i, e: (0, 0)),
                  pl.BlockSpec((1, _ROUTER_PAD), lambda i, e: (0, 0)),
                  pl.BlockSpec((None, D_MODEL, EXPERT_HIDDEN), lambda i, e: (e, 0, 0)),
                  pl.BlockSpec((None, D_MODEL, EXPERT_HIDDEN), lambda i, e: (e, 0, 0)),
                  pl.BlockSpec((None, EXPERT_HIDDEN, D_MODEL), lambda i, e: (e, 0, 0)),
                  vec, vec],
        out_specs=row,
        out_shape=jax.ShapeDtypeStruct((n, D_MODEL), F32),
        scratch_shapes=[pltpu.VMEM((tm, D_MODEL), BF16), pltpu.VMEM((tm, _ROUTER_PAD), F32),
                        pltpu.VMEM((tm, D_MODEL), F32)],
        compiler_params=_cparams(("parallel", "arbitrary")),
        name="moe",
    )(h, wr, br, wg, wu, wd, ln_g[None], ln_b[None])


def _prep_in_proj(w_in, b_in):
    c_gn = 2 * D_MODEL + D_MODEL + KV_COLS
    n_gn = 3 * N_HEADS
    pad = GATE_PAD - n_gn

    def rearr(a):
        z = jnp.zeros(a.shape[:-1] + (pad,), a.dtype)
        return jnp.concatenate([a[..., :c_gn], a[..., c_gn + n_gn:], a[..., c_gn:c_gn + n_gn], z], axis=-1)

    return rearr(w_in).astype(BF16), rearr(b_in)[:, None, :]


def _prep_cmp(cmp_pe, cmp_w):
    depth = cmp_w.shape[0]
    w = cmp_w.reshape(depth, 2, CMP_SUB, CMP_STRIDE, HEAD_DIM, HEAD_DIM)
    eye = jnp.eye(N_KV, dtype=cmp_w.dtype)
    wbd = jnp.einsum('zshlde,gk->zslgdhke', w, eye).reshape(depth, 2, _CMP_K, _CMP_N).astype(BF16)
    pe = cmp_pe.reshape(depth, 2, CMP_SUB, CMP_STRIDE, 1, HEAD_DIM)
    pe = jnp.broadcast_to(pe, (depth, 2, CMP_SUB, CMP_STRIDE, N_KV, HEAD_DIM)).reshape(depth, 2, CMP_SUB, _CMP_K)
    pe8 = jnp.concatenate([pe, jnp.zeros((depth, 2, SUBLANES - CMP_SUB, _CMP_K), pe.dtype)], axis=2)
    return pe8, wbd


def _deinterleave_perm():
    cpp = PAGE_SIZE // CMP_STRIDE
    r = np.arange(PAGE_SIZE)
    src = (r % cpp) * CMP_STRIDE + r // cpp
    return jnp.asarray(src[:, None] == np.arange(PAGE_SIZE)[None, :], BF16)


def kernel(x_prompt, x_sample, cache_kv, state_win, state_conv, page_table, w_in, b_in, conv_w, conv_b, conv_ln_g, conv_ln_b, w_up_conv, cmp_pe, cmp_w, w_up_nsa, gm_ln_g, gm_ln_b, gm_ws, gm_bs, w_up_gm, w_o, ln_g, ln_b, router_g_w, router_g_b, router_e_w, router_e_b, moe_w_gate, moe_w_up, moe_w_down):
    depth = w_in.shape[0]
    batch, seq, _ = x_prompt.shape
    dec_b, t_new, _ = x_sample.shape
    n_pages = page_table.shape[1]
    past = n_pages * PAGE_SIZE
    win_buf = state_win.shape[2]
    alpha = float((2 * depth) ** 0.25)
    assert seq % _KC == 0 and seq % GM_CHUNK == 0 and seq >= WINDOW
    assert t_new < CMP_STRIDE and t_new <= _NEW_PAD and past % SEL_BLOCK == 0 and win_buf == min(WINDOW, past)

    w_in_r, b_in_r = _prep_in_proj(w_in, b_in)
    c_kv = 2 * D_MODEL + D_MODEL
    v_cols = np.concatenate([np.arange(c_kv + 3 * KV_W, c_kv + 4 * KV_W), np.arange(c_kv + 5 * KV_W, c_kv + 6 * KV_W)])
    wvt = w_in[:, :, v_cols].transpose(0, 2, 1).astype(BF16)
    bvt = jnp.broadcast_to(b_in[:, v_cols, None], (depth, 2 * KV_W, LANES))
    pe8, wbd = _prep_cmp(cmp_pe, cmp_w)
    wc16, wn16, wg16, wo16 = (w.astype(BF16) for w in (w_up_conv, w_up_nsa, w_up_gm, w_o))
    mg16, mu16, md16 = moe_w_gate.astype(BF16), moe_w_up.astype(BF16), moe_w_down.astype(BF16)
    zr = jnp.zeros((depth, D_MODEL, _ROUTER_PAD - N_GROUPS - N_EXPERTS), F32)
    w_router = jnp.concatenate([router_g_w, router_e_w.reshape(depth, D_MODEL, N_EXPERTS), zr], axis=-1)
    zb = jnp.zeros((depth, _ROUTER_PAD - N_GROUPS - N_EXPERTS), F32)
    b_router = jnp.concatenate([router_g_b, router_e_b.reshape(depth, N_EXPERTS), zb], axis=-1)[:, None, :]
    gw = D_MODEL // GM_GROUPS
    wrow = jnp.repeat(gm_ws[:, :, :t_new, :t_new].transpose(0, 2, 3, 1).reshape(depth, t_new * t_new, GM_GROUPS), gw, axis=-1)
    bsrow = jnp.repeat(gm_bs[:, :, :t_new].transpose(0, 2, 1), gw, axis=-1)

    cache_t = cache_kv.transpose(0, 1, 3, 4, 5, 2).reshape(depth, cache_kv.shape[1], N_KV_SLOTS * KV_W, PAGE_SIZE)
    win_t_all = state_win.transpose(0, 1, 3, 4, 5, 2).reshape(depth, dec_b, 2 * KV_W, win_buf)

    kvc_s_all = _cmp_sample(page_table, cache_t, _deinterleave_perm(), pe8, wbd)

    xp = x_prompt.reshape(batch * seq, D_MODEL)
    xs = x_sample.reshape(dec_b * t_new, D_MODEL)
    kvp, kvs, wnp, wns, cvp, cvs, gmv = [], [], [], [], [], [], []
    wb = min(WINDOW, seq)
    eye_g = jnp.eye(N_KV, dtype=BF16)
    for l in range(depth):
        glu, q, kv4, kwn, kvh, u, v, gm, gn, vt = _in_proj(xp, w_in_r[l], b_in_r[l], wvt[l], bvt[l])
        ca = _conv_prompt(glu, conv_w[l], conv_b[l], conv_ln_g[l], conv_ln_b[l], batch, seq)
        gc = _gmlp_prompt(u, v, gm_ln_g[l], gm_ln_b[l], gm_ws[l], gm_bs[l])
        kvc = _cmp_prompt(kv4, pe8[l], wbd[l], batch, seq)
        on = _nsa_prompt(q, gn, kvc, kvh, vt, batch, seq)
        h = _merge(xp, ca, on, gc, gm, wc16[l], wn16[l], wg16[l], wo16[l], ln_g[l, 0], ln_b[l, 0], alpha)
        xp = _moe(h, w_router[l], b_router[l], mg16[l], mu16[l], md16[l], ln_g[l, 1], ln_b[l, 1], alpha)
        kvp.append(kv4.reshape(batch, seq, N_KV_SLOTS, N_KV, HEAD_DIM))
        wnp.append(kwn.reshape(batch, seq, 2 * KV_W)[:, seq - wb:].reshape(batch, wb, 2, N_KV, HEAD_DIM))
        cvp.append(glu.reshape(batch, seq, D_MODEL)[:, seq - (CONV_WIDTH - 1):])

        glu, q, kv4, kwn, kvh, u, v, gm, gn, _ = _in_proj(xs, w_in_r[l], b_in_r[l], wvt[l], bvt[l])
        conv_ctx = jnp.concatenate([state_conv[l], glu.reshape(dec_b, t_new, D_MODEL)], axis=1)
        tmaj = lambda a: a.reshape(dec_b, t_new, D_MODEL).transpose(1, 0, 2)
        ca_t, gc_t, vn_t = _sample_seq(conv_ctx.transpose(1, 0, 2), tmaj(u), tmaj(v), conv_w[l], conv_b[l],
                                       conv_ln_g[l], conv_ln_b[l], gm_ln_g[l], gm_ln_b[l], wrow[l], bsrow[l])
        bmaj = lambda a: a.transpose(1, 0, 2).reshape(dec_b * t_new, D_MODEL)
        q5 = q.reshape(dec_b, t_new, N_KV, Q_PER_KV, HEAD_DIM).transpose(0, 2, 1, 3, 4)
        qbd = jnp.einsum('bgtrd,gk->bgtrkd', q5, eye_g).reshape(dec_b, N_KV * t_new * Q_PER_KV, KV_W)
        g5 = gn[:, :3 * N_HEADS].reshape(dec_b, t_new, N_KV, Q_PER_KV, 3).transpose(0, 2, 1, 3, 4)
        gates = jnp.pad(g5.reshape(dec_b, N_KV * t_new * Q_PER_KV, 3), ((0, 0), (0, 0), (0, LANES - 3)))
        new_rows = jnp.pad(kvh.reshape(dec_b, t_new, 4 * KV_W), ((0, 0), (0, _NEW_PAD - t_new), (0, 0)))
        o5 = _nsa_sample(page_table, qbd, gates, kvc_s_all, new_rows, win_t_all, cache_t, l, t_new)
        on = o5.reshape(dec_b, N_KV, t_new, Q_PER_KV, HEAD_DIM).transpose(0, 2, 1, 3, 4)
        on = on.reshape(dec_b * t_new, D_MODEL).astype(BF16)
        h = _merge(xs, bmaj(ca_t), on, bmaj(gc_t), gm, wc16[l], wn16[l], wg16[l], wo16[l], ln_g[l, 0], ln_b[l, 0], alpha)
        xs = _moe(h, w_router[l], b_router[l], mg16[l], mu16[l], md16[l], ln_g[l, 1], ln_b[l, 1], alpha)
        kvs.append(kv4.reshape(dec_b, t_new, N_KV_SLOTS, N_KV, HEAD_DIM))
        wns.append(kwn.reshape(dec_b, t_new, 2, N_KV, HEAD_DIM))
        cvs.append(conv_ctx[:, t_new:])
        gmv.append(vn_t.transpose(1, 0, 2))

    win_sample = jnp.concatenate([state_win[:, :, t_new:], jnp.stack(wns)], axis=2)
    return (xp.reshape(batch, seq, D_MODEL), xs.reshape(dec_b, t_new, D_MODEL), jnp.stack(kvp), jnp.stack(kvs),
            jnp.stack(wnp), win_sample, jnp.stack(cvp), jnp.stack(cvs), jnp.stack(gmv))
```

```python
import functools

import jax
import jax.numpy as jnp
import numpy as np
from jax import lax
from jax.experimental import pallas as pl
from jax.experimental.pallas import tpu as pltpu

F32 = jnp.float32
BF16 = jnp.bfloat16

D_MODEL = 1024
CONV_WIDTH = 31
N_HEADS = 16
N_KV = 4
HEAD_DIM = 64
Q_PER_KV = N_HEADS // N_KV
CMP_BLOCK = 32
CMP_STRIDE = 16
CMP_SUB = CMP_BLOCK // CMP_STRIDE
SEL_BLOCK = 64
SEL_SHIFT = 6
SEL_TOPN = 8
WINDOW = 512
PAGE_SIZE = 128
N_KV_SLOTS = 4
FORCE_BONUS = 1.0e4
GM_GROUPS = 4
GM_CHUNK = 128
N_GROUPS = 4
EXPERTS_PER_GROUP = 4
EPG_SHIFT = 2
N_EXPERTS = N_GROUPS * EXPERTS_PER_GROUP
EXPERT_HIDDEN = 512
LN_EPS = 1e-5
NEG = -1e30
KV_W = N_KV * HEAD_DIM
KV_COLS = 6 * KV_W
GATE_PAD = 128

V7X_VMEM_BYTES = 64 * 1024 * 1024
VMEM_LIMIT = V7X_VMEM_BYTES * 7 // 8
LANES = 128
SUBLANES = 8

HIGHEST = lax.Precision.HIGHEST


def _cparams(sem):
    return pltpu.CompilerParams(dimension_semantics=sem, vmem_limit_bytes=VMEM_LIMIT)


def _pick(n, cands):
    for c in cands:
        if n % c == 0:
            return c
    raise ValueError(f"no tile in {cands} divides {n}")


def _sigmoid(x):
    return 1.0 / (1.0 + jnp.exp(-x))


def _gelu_tanh(x):
    return 0.5 * x * (1.0 + jnp.tanh(np.sqrt(2.0 / np.pi).astype(np.float32) * (x + 0.044715 * (x * x * x))))


def _layer_norm(x, g, b):
    mu = jnp.mean(x, axis=-1, keepdims=True)
    xc = x - mu
    var = jnp.mean(xc * xc, axis=-1, keepdims=True)
    return xc * lax.rsqrt(var + LN_EPS) * g + b


def _dot(a, b):
    return jnp.dot(a, b, preferred_element_type=F32)


def _dot_nt(a, b, precision=None):
    return lax.dot_general(a, b, (((1,), (1,)), ((), ())), preferred_element_type=F32, precision=precision)


def _dot_tn(a, b):
    return lax.dot_general(a, b, (((0,), (0,)), ((), ())), preferred_element_type=F32)


def _masked_softmax_rows(s, mask):
    s = jnp.where(mask, s, NEG)
    m = jnp.max(s, axis=-1, keepdims=True)
    e = jnp.exp(s - m)
    p = e / jnp.sum(e, axis=-1, keepdims=True)
    return jnp.where(mask, p, 0.0)


_C_A = 0
_C_GATE = _C_A + D_MODEL
_C_Q = _C_GATE + D_MODEL
_C_KV = _C_Q + D_MODEL
_C_U = _C_KV + KV_COLS
_C_V = _C_U + D_MODEL
_C_GM = _C_V + D_MODEL
_C_GN = _C_GM + 3 * D_MODEL
IN_COLS = _C_GN + GATE_PAD


def _in_proj_kernel(x_ref, w_ref, b_ref, wvt_ref, bvt_ref, glu_ref, q_ref, kv4_ref, kwn_ref, kvh_ref, u_ref, v_ref, gm_ref,
                    gn_ref, vt_ref):
    x = x_ref[...].astype(BF16)

    def seg(lo, hi):
        return _dot(x, w_ref[:, lo:hi]) + b_ref[:, lo:hi]

    glu_ref[...] = seg(_C_A, _C_GATE) * _sigmoid(seg(_C_GATE, _C_Q))
    q_ref[...] = (seg(_C_Q, _C_KV) * (HEAD_DIM ** -0.5)).astype(BF16)
    kv = seg(_C_KV, _C_U)
    kv4_ref[...] = kv[:, :N_KV_SLOTS * KV_W]
    kwn_ref[...] = kv[:, N_KV_SLOTS * KV_W:]
    kvh_ref[...] = kv[:, 2 * KV_W:].astype(BF16)
    u_ref[...] = _gelu_tanh(seg(_C_U, _C_V)).astype(BF16)
    v_ref[...] = _gelu_tanh(seg(_C_V, _C_GM))
    gm_ref[...] = _sigmoid(seg(_C_GM, _C_GN))
    gn_ref[...] = _sigmoid(seg(_C_GN, IN_COLS))
    vt_ref[...] = (_dot_nt(wvt_ref[...], x) + bvt_ref[:, 0:1]).astype(BF16)


def _in_proj(x, w_r, b_r, wvt, bvt, layer):
    n = x.shape[0]
    tm = _pick(n, (256, 128, 64, 32, 16, 8))
    row = lambda w: pl.BlockSpec((tm, w), lambda i: (i, 0))
    return pl.pallas_call(
        _in_proj_kernel,
        grid=(n // tm,),
        in_specs=[row(D_MODEL),
                  pl.BlockSpec((None, D_MODEL, IN_COLS), lambda i: (layer, 0, 0), pipeline_mode=pl.Buffered(1)),
                  pl.BlockSpec((None, 1, IN_COLS), lambda i: (layer, 0, 0)),
                  pl.BlockSpec((None, 2 * KV_W, D_MODEL), lambda i: (layer, 0, 0)),
                  pl.BlockSpec((None, 2 * KV_W, LANES), lambda i: (layer, 0, 0))],
        out_specs=[row(D_MODEL), row(D_MODEL), row(N_KV_SLOTS * KV_W), row(2 * KV_W), row(4 * KV_W), row(D_MODEL),
                   row(D_MODEL), row(3 * D_MODEL), row(GATE_PAD),
                   pl.BlockSpec((2 * KV_W, tm), lambda i: (0, i))],
        out_shape=[jax.ShapeDtypeStruct((n, D_MODEL), F32),
                   jax.ShapeDtypeStruct((n, D_MODEL), BF16),
                   jax.ShapeDtypeStruct((n, N_KV_SLOTS * KV_W), F32),
                   jax.ShapeDtypeStruct((n, 2 * KV_W), F32),
                   jax.ShapeDtypeStruct((n, 4 * KV_W), BF16),
                   jax.ShapeDtypeStruct((n, D_MODEL), BF16),
                   jax.ShapeDtypeStruct((n, D_MODEL), F32),
                   jax.ShapeDtypeStruct((n, 3 * D_MODEL), F32),
                   jax.ShapeDtypeStruct((n, GATE_PAD), F32),
                   jax.ShapeDtypeStruct((2 * KV_W, n), BF16)],
        compiler_params=_cparams(("parallel",)),
        name="in_proj",
    )(x, w_r, b_r, wvt, bvt)


_CONV_HALO = 32
_CONV_ROWS = SUBLANES
_CONV_LEAD = _CONV_HALO - (CONV_WIDTH - 1)
_CONV_SHIFT_EXTRA = 24


def _conv_prompt_kernel(halo_ref, glu_ref, w_ref, cb_ref, g_ref, b_ref, out_ref, win_ref, sh_ref, y_ref, *, tq):
    i = pl.program_id(1)
    win_ref[0:_CONV_HALO, :] = jnp.where(i > 0, halo_ref[...], 0.0)
    win_ref[_CONV_HALO:_CONV_HALO + tq, :] = glu_ref[...]
    n_sh = tq + _CONV_SHIFT_EXTRA
    for p in range(1, SUBLANES):
        sh_ref[p, 0:n_sh, :] = win_ref[p:p + n_sh, :]

    def chunk(c, carry):
        r0 = pl.multiple_of(c * _CONV_ROWS, _CONV_ROWS)
        acc = jnp.zeros((_CONV_ROWS, D_MODEL), F32)
        for k in range(CONV_WIDTH):
            phase, base = (_CONV_LEAD + k) % SUBLANES, (_CONV_LEAD + k) // SUBLANES * SUBLANES
            if phase == 0:
                x = win_ref[pl.ds(r0 + base, _CONV_ROWS), :]
            else:
                x = sh_ref[phase, pl.ds(r0 + base, _CONV_ROWS), :]
            acc = acc + x * w_ref[k]
        y_ref[pl.ds(r0, _CONV_ROWS), :] = acc
        return carry

    lax.fori_loop(0, tq // _CONV_ROWS, chunk, 0)
    y = _layer_norm(y_ref[...] + cb_ref[...], g_ref[...], b_ref[...])
    out_ref[...] = (y * _sigmoid(y)).astype(BF16)


def _conv_prompt(glu, conv_w, conv_b, ln_g, ln_b, batch, seq):
    tq = _pick(seq, (256, 128))
    glu3 = glu.reshape(batch, seq, D_MODEL)
    hb = tq // _CONV_HALO
    vec = pl.BlockSpec((1, D_MODEL), lambda b, i: (0, 0))
    out = pl.pallas_call(
        functools.partial(_conv_prompt_kernel, tq=tq),
        grid=(batch, seq // tq),
        in_specs=[pl.BlockSpec((None, _CONV_HALO, D_MODEL), lambda b, i: (b, jnp.maximum(i * hb - 1, 0), 0)),
                  pl.BlockSpec((None, tq, D_MODEL), lambda b, i: (b, i, 0)),
                  pl.BlockSpec((CONV_WIDTH, SUBLANES, D_MODEL), lambda b, i: (0, 0, 0)),
                  vec, vec, vec],
        out_specs=pl.BlockSpec((None, tq, D_MODEL), lambda b, i: (b, i, 0)),
        out_shape=jax.ShapeDtypeStruct((batch, seq, D_MODEL), BF16),
        scratch_shapes=[pltpu.VMEM((_CONV_HALO + tq, D_MODEL), F32),
                        pltpu.VMEM((SUBLANES, tq + _CONV_SHIFT_EXTRA, D_MODEL), F32),
                        pltpu.VMEM((tq, D_MODEL), F32)],
        compiler_params=_cparams(("parallel", "parallel")),
        name="conv_prompt",
    )(glu3, glu3, jnp.broadcast_to(conv_w[:, None, :], (CONV_WIDTH, SUBLANES, D_MODEL)), conv_b[None], ln_g[None],
      ln_b[None])
    return out.reshape(batch * seq, D_MODEL)


def _gmlp_prompt_kernel(u_ref, v_ref, g_ref, b_ref, ws_ref, bs_ref, out_ref):
    vn = _layer_norm(v_ref[...], g_ref[...], b_ref[...]).astype(BF16)
    ri = lax.broadcasted_iota(jnp.int32, (GM_CHUNK, GM_CHUNK), 0)
    ci = lax.broadcasted_iota(jnp.int32, (GM_CHUNK, GM_CHUNK), 1)
    gw = D_MODEL // GM_GROUPS
    for g in range(GM_GROUPS):
        w = jnp.where(ri >= ci, ws_ref[g], 0.0).astype(BF16)
        mixed = _dot(w, vn[:, g * gw:(g + 1) * gw]) + bs_ref[:, g:g + 1]
        out_ref[:, g * gw:(g + 1) * gw] = (u_ref[:, g * gw:(g + 1) * gw].astype(F32) * mixed).astype(BF16)


def _gmlp_prompt(u, v, ln_g, ln_b, gm_ws, gm_bs):
    n = u.shape[0]
    vec = pl.BlockSpec((1, D_MODEL), lambda i: (0, 0))
    row = pl.BlockSpec((GM_CHUNK, D_MODEL), lambda i: (i, 0))
    return pl.pallas_call(
        _gmlp_prompt_kernel,
        grid=(n // GM_CHUNK,),
        in_specs=[row, row, vec, vec,
                  pl.BlockSpec((GM_GROUPS, GM_CHUNK, GM_CHUNK), lambda i: (0, 0, 0)),
                  pl.BlockSpec((GM_CHUNK, GM_GROUPS), lambda i: (0, 0))],
        out_specs=row,
        out_shape=jax.ShapeDtypeStruct((n, D_MODEL), BF16),
        compiler_params=_cparams(("parallel",)),
        name="gmlp_prompt",
    )(u, v, ln_g[None], ln_b[None], gm_ws, gm_bs.T)


def _sample_seq_kernel(ctx_ref, u_ref, v_ref, cw_ref, cb_ref, cg_ref, cbb_ref, gg_ref, gb_ref, wrow_ref, bsrow_ref,
                       ca_ref, gc_ref, vn_ref, *, t_new):
    for t in range(t_new):
        acc = ctx_ref[t] * cw_ref[0:1, :]
        for k in range(1, CONV_WIDTH):
            acc = acc + ctx_ref[t + k] * cw_ref[k:k + 1, :]
        y = _layer_norm(acc + cb_ref[...], cg_ref[...], cbb_ref[...])
        ca_ref[t] = (y * _sigmoid(y)).astype(BF16)
    for t in range(t_new):
        vn_ref[t] = _layer_norm(v_ref[t], gg_ref[...], gb_ref[...])
    for i in range(t_new):
        mixed = bsrow_ref[i:i + 1, :]
        for j in range(i + 1):
            mixed = mixed + wrow_ref[i * t_new + j:i * t_new + j + 1, :] * vn_ref[j]
        gc_ref[i] = (u_ref[i].astype(F32) * mixed).astype(BF16)


def _sample_seq(ctx_t, u_t, v_t, conv_w, conv_b, cg, cb, gg, gb, wrow, bsrow):
    rows, dec_b, _ = ctx_t.shape
    t_new = u_t.shape[0]
    sb = _pick(dec_b, (32, 16, 8))
    vec = pl.BlockSpec((1, D_MODEL), lambda i: (0, 0))
    blk = lambda r: pl.BlockSpec((r, sb, D_MODEL), lambda i: (0, i, 0))
    full = lambda r: pl.BlockSpec((r, D_MODEL), lambda i: (0, 0))
    return pl.pallas_call(
        functools.partial(_sample_seq_kernel, t_new=t_new),
        grid=(dec_b // sb,),
        in_specs=[blk(rows), blk(t_new), blk(t_new), full(CONV_WIDTH), vec, vec, vec, vec, vec,
                  full(t_new * t_new), full(t_new)],
        out_specs=[blk(t_new), blk(t_new), blk(t_new)],
        out_shape=[jax.ShapeDtypeStruct((t_new, dec_b, D_MODEL), BF16),
                   jax.ShapeDtypeStruct((t_new, dec_b, D_MODEL), BF16),
                   jax.ShapeDtypeStruct((t_new, dec_b, D_MODEL), F32)],
        compiler_params=_cparams(("parallel",)),
        name="sample_seq",
    )(ctx_t, u_t, v_t, conv_w, conv_b[None], cg[None], cb[None], gg[None], gb[None], wrow, bsrow)


_CMP_K = CMP_STRIDE * KV_W
_CMP_N = CMP_SUB * KV_W


def _combine_halves(y, bias):
    first = y[:, :KV_W] + bias[0:1, :KV_W]
    second = y[:, KV_W:] + bias[1:2, KV_W:]
    return first + pltpu.roll(second, shift=y.shape[0] - 1, axis=0)


def _cmp_prompt_kernel(lo_ref, hi_ref, pe_ref, w_ref, out_ref, *, n_chunks):
    bias = _dot(pe_ref[...].astype(BF16), w_ref[...])
    y = jnp.zeros((n_chunks, _CMP_N), F32)
    for l in range(CMP_STRIDE):
        rows = jnp.concatenate([lo_ref[pl.ds(l, n_chunks, stride=CMP_STRIDE), :],
                                hi_ref[pl.ds(l, n_chunks, stride=CMP_STRIDE), :]], axis=-1).astype(BF16)
        y = y + _dot(rows, w_ref[l * KV_W:(l + 1) * KV_W, :])
    out_ref[...] = _combine_halves(y, bias)


def _cmp_prompt(kv4, pe8, wbd, batch, seq, layer):
    n_chunks = seq // CMP_STRIDE
    kv3 = kv4.reshape(batch, seq, N_KV_SLOTS * KV_W)
    return pl.pallas_call(
        functools.partial(_cmp_prompt_kernel, n_chunks=n_chunks),
        grid=(batch, 2),
        in_specs=[pl.BlockSpec((None, seq, LANES), lambda b, s: (b, 0, 2 * s)),
                  pl.BlockSpec((None, seq, LANES), lambda b, s: (b, 0, 2 * s + 1)),
                  pl.BlockSpec((None, None, SUBLANES, _CMP_K), lambda b, s: (layer, s, 0, 0)),
                  pl.BlockSpec((None, None, _CMP_K, _CMP_N), lambda b, s: (layer, s, 0, 0))],
        out_specs=pl.BlockSpec((None, None, n_chunks, KV_W), lambda b, s: (b, s, 0, 0)),
        out_shape=jax.ShapeDtypeStruct((batch, 2, n_chunks, KV_W), F32),
        compiler_params=_cparams(("parallel", "parallel")),
        name="cmp_prompt",
    )(kv3, kv3, pe8, wbd)


def _cmp_page_copy(cache_ref, buf_ref, sem_ref, pt_ref, layer, sample, page, slot, k, n_pages):
    pid = pt_ref[sample * n_pages + page]
    return pltpu.make_async_copy(cache_ref.at[layer, pid, pl.ds(0, 2 * KV_W), :], buf_ref.at[slot, k], sem_ref.at[slot])


def _cmp_sample_kernel(pt_ref, perm_ref, pe_ref, w_ref, cache_ref, out_ref, buf_ref, rk_ref, rv_ref, sem_ref,
                       *, sb, n_pages, steps_per_layer, n_steps):
    step = pl.program_id(0)
    slot = step % 2
    per_step = sb * n_pages

    def issue(st, sl):
        layer = st // steps_per_layer
        s0 = (st % steps_per_layer) * sb

        def body(k, carry):
            _cmp_page_copy(cache_ref, buf_ref, sem_ref, pt_ref, layer, s0 + k // n_pages, k % n_pages, sl, k,
                           n_pages).start()
            return carry

        lax.fori_loop(0, per_step, body, 0)

    @pl.when(step == 0)
    def _():
        issue(step, slot)

    @pl.when(step + 1 < n_steps)
    def _():
        issue(step + 1, 1 - slot)

    def wait_body(k, carry):
        _cmp_page_copy(cache_ref, buf_ref, sem_ref, pt_ref, 0, 0, 0, slot, k, n_pages).wait()
        return carry

    lax.fori_loop(0, per_step, wait_body, 0)

    chunks_per_page = PAGE_SIZE // CMP_STRIDE

    def page_body(k, carry):
        a = buf_ref[slot, k].astype(BF16)
        t = _dot_nt(perm_ref[...], a)
        r0 = pl.multiple_of(k * chunks_per_page, chunks_per_page)
        for l in range(CMP_STRIDE):
            piece = t[l * chunks_per_page:(l + 1) * chunks_per_page, :]
            rk_ref[pl.ds(r0, chunks_per_page), l * KV_W:(l + 1) * KV_W] = piece[:, :KV_W]
            rv_ref[pl.ds(r0, chunks_per_page), l * KV_W:(l + 1) * KV_W] = piece[:, KV_W:]
        return carry

    lax.fori_loop(0, per_step, page_body, 0, unroll=_pick(per_step, (4, 2, 1)))

    n_chunks = n_pages * chunks_per_page
    for s, r_ref in ((0, rk_ref), (1, rv_ref)):
        bias = _dot(pe_ref[s].astype(BF16), w_ref[s])
        y = _dot(r_ref[...].astype(BF16), w_ref[s])
        blocks = _combine_halves(y, bias)
        for j in range(sb):
            out_ref[j, s] = blocks[j * n_chunks:(j + 1) * n_chunks]


def _cmp_sample(page_table, cache_t, perm, pe8, wbd):
    depth = cache_t.shape[0]
    dec_b, n_pages = page_table.shape
    sb = _pick(dec_b, (2, 1))
    steps_per_layer = dec_b // sb
    n_steps = depth * steps_per_layer
    n_chunks = n_pages * (PAGE_SIZE // CMP_STRIDE)
    rows = sb * n_chunks
    grid_spec = pltpu.PrefetchScalarGridSpec(
        num_scalar_prefetch=1,
        grid=(n_steps,),
        in_specs=[pl.BlockSpec((PAGE_SIZE, PAGE_SIZE), lambda i, pt: (0, 0)),
                  pl.BlockSpec((None, 2, SUBLANES, _CMP_K), lambda i, pt: (i // steps_per_layer, 0, 0, 0)),
                  pl.BlockSpec((None, 2, _CMP_K, _CMP_N), lambda i, pt: (i // steps_per_layer, 0, 0, 0)),
                  pl.BlockSpec(memory_space=pl.ANY)],
        out_specs=pl.BlockSpec((sb, 2, n_chunks, KV_W), lambda i, pt: (i, 0, 0, 0)),
        scratch_shapes=[pltpu.VMEM((2, sb * n_pages, 2 * KV_W, PAGE_SIZE), F32),
                        pltpu.VMEM((rows, _CMP_K), F32),
                        pltpu.VMEM((rows, _CMP_K), F32),
                        pltpu.SemaphoreType.DMA((2,))])
    out = pl.pallas_call(
        functools.partial(_cmp_sample_kernel, sb=sb, n_pages=n_pages, steps_per_layer=steps_per_layer,
                          n_steps=n_steps),
        grid_spec=grid_spec,
        out_shape=jax.ShapeDtypeStruct((depth * dec_b, 2, n_chunks, KV_W), F32),
        compiler_params=_cparams(("arbitrary",)),
        name="cmp_sample",
    )(page_table.reshape(-1), perm, pe8, wbd, cache_t)
    return out.reshape(depth, dec_b, 2, n_chunks, KV_W)


_TQ = 256
_KC = 256


def _rank_select(score, blk, n_blocks, axis):
    rank = jnp.zeros(score.shape, F32)
    for sp in range(n_blocks):
        row = lax.slice_in_dim(score, sp, sp + 1, axis=axis)
        before = jnp.where(blk > sp, 1.0, 0.0)
        rank = rank + jnp.where(row > score, 1.0, jnp.where(row == score, before, 0.0))
    return jnp.where(rank < float(min(SEL_TOPN, n_blocks)), 1.0, 0.0)


def _nsa_prompt_kernel(q_ref, gnt_ref, kvc_ref, kh_ref, vt_ref, mt_ref, o_ref, bias_ref, *, seq, n_cmp, n_sel, n_sel_pad):
    i = pl.program_id(1)
    t0 = i * _TQ
    cols = Q_PER_KV * _TQ
    n_chunks = kvc_ref.shape[1]
    win_keys = min(WINDOW + _TQ, seq)
    bpc = _KC // SEL_BLOCK

    pos = t0 + (lax.broadcasted_iota(jnp.int32, (1, cols), 1) & (_TQ - 1))
    pos_t = t0 + lax.broadcasted_iota(jnp.int32, (1, _TQ), 1)

    n_idx = lax.broadcasted_iota(jnp.int32, (n_chunks, 1), 0)
    mask_c = (n_idx * CMP_STRIDE + (CMP_BLOCK - 1) <= pos) & (n_idx < n_cmp)

    blk = lax.broadcasted_iota(jnp.int32, (n_sel_pad, _TQ), 0)
    cur = jnp.right_shift(pos_t, SEL_SHIFT)
    valid = (blk <= cur) & (blk < n_sel)
    forced = (blk == 0) | (blk == cur) | (blk == cur - 1)

    w_start = pl.multiple_of(jnp.maximum(t0 + _TQ - win_keys, 0), _TQ)
    wpos = w_start + lax.broadcasted_iota(jnp.int32, (win_keys, 1), 0)
    bias_w = jnp.where((wpos <= pos) & (wpos > pos - WINDOW), 0.0, NEG)
    n_kc = (t0 + _TQ + _KC - 1) // _KC
    kidx = lax.broadcasted_iota(jnp.int32, (_KC, 1), 0)

    qgs, o_cs = [], []
    for g in range(N_KV):
        lo = g * HEAD_DIM
        qg = jnp.concatenate([q_ref[:, (g * Q_PER_KV + r) * HEAD_DIM:(g * Q_PER_KV + r + 1) * HEAD_DIM]
                              for r in range(Q_PER_KV)], axis=0)
        qgs.append(qg)

        kc = kvc_ref[0, :, lo:lo + HEAD_DIM].astype(BF16)
        vc = kvc_ref[1, :, lo:lo + HEAD_DIM].astype(BF16)
        s_c = jnp.where(mask_c, _dot_nt(kc, qg), NEG)
        e_c = jnp.exp(s_c - jnp.max(s_c, axis=0, keepdims=True))
        p_c = jnp.where(mask_c, e_c / jnp.sum(e_c, axis=0, keepdims=True), 0.0)
        o_cs.append(_dot_tn(vc, p_c.astype(BF16)))

        psum = p_c[:, 0:_TQ]
        for r in range(1, Q_PER_KV):
            psum = psum + p_c[:, r * _TQ:(r + 1) * _TQ]
        imp = jnp.dot(mt_ref[...], psum, precision=HIGHEST, preferred_element_type=F32)
        score = jnp.where(valid, imp + jnp.where(forced, FORCE_BONUS, 0.0), -FORCE_BONUS)
        score = jnp.where(blk < n_sel, score, -jnp.inf)
        sel = _rank_select(score, blk, n_sel, axis=0)
        bias_sel = jnp.where(sel > 0.5, 0.0, NEG)
        for c in range(seq // _KC):
            bias_ref[g, c, 0:bpc, :] = bias_sel[c * bpc:(c + 1) * bpc, :]

    def sel_chunk(c, carry, causal):
        k0 = pl.multiple_of(c * _KC, _KC)
        out = []
        for g in range(N_KV):
            lo = g * HEAD_DIM
            m, l, acc = carry[g]
            ks = kh_ref[pl.ds(k0, _KC), lo:lo + HEAD_DIM]
            vst = vt_ref[lo:lo + HEAD_DIM, pl.ds(k0, _KC)]
            bb = bias_ref[g, c, 0:bpc, :]
            bias = jnp.concatenate([jnp.broadcast_to(bb[j:j + 1, :], (SEL_BLOCK, _TQ)) for j in range(bpc)], axis=0)
            s = _dot_nt(ks, qgs[g])
            s = jnp.concatenate([s[:, r * _TQ:(r + 1) * _TQ] + bias for r in range(Q_PER_KV)], axis=1)
            if causal:
                s = jnp.where(k0 + kidx <= pos, s, NEG)
            m_new = jnp.maximum(m, jnp.max(s, axis=0, keepdims=True))
            alpha = jnp.exp(m - m_new)
            p = jnp.exp(s - m_new)
            l = alpha * l + jnp.sum(p, axis=0, keepdims=True)
            acc = alpha * acc + _dot(vst, p.astype(BF16))
            out.append((m_new, l, acc))
        return tuple(out)

    carry = tuple((jnp.full((1, cols), NEG, F32), jnp.zeros((1, cols), F32), jnp.zeros((HEAD_DIM, cols), F32))
                  for _ in range(N_KV))
    carry = lax.fori_loop(0, n_kc - 1, functools.partial(sel_chunk, causal=False), carry)
    carry = sel_chunk(n_kc - 1, carry, causal=True)

    for g in range(N_KV):
        lo = g * HEAD_DIM
        qg, o_c = qgs[g], o_cs[g]
        _, l_s, acc_s = carry[g]
        o_s = acc_s * jnp.where(l_s > 0.0, 1.0 / l_s, 0.0)

        kw = kh_ref[pl.ds(w_start, win_keys), 2 * KV_W + lo:2 * KV_W + lo + HEAD_DIM]
        vwt = vt_ref[KV_W + lo:KV_W + lo + HEAD_DIM, pl.ds(w_start, win_keys)]
        s_w = _dot_nt(kw, qg) + bias_w
        e_w = jnp.exp(s_w - jnp.max(s_w, axis=0, keepdims=True))
        o_w = _dot(vwt, e_w.astype(BF16)) * (1.0 / jnp.sum(e_w, axis=0, keepdims=True))

        for r in range(Q_PER_KV):
            h = g * Q_PER_KV + r
            sl = slice(r * _TQ, (r + 1) * _TQ)
            o = (gnt_ref[3 * h:3 * h + 1, :] * o_c[:, sl] + gnt_ref[3 * h + 1:3 * h + 2, :] * o_s[:, sl]
                 + gnt_ref[3 * h + 2:3 * h + 3, :] * o_w[:, sl])
            o_ref[h * HEAD_DIM:(h + 1) * HEAD_DIM, :] = o.astype(BF16)


def _nsa_prompt(q, gn, kvc, kvh, vt, batch, seq):
    n_chunks = seq // CMP_STRIDE
    n_cmp = (seq - CMP_BLOCK) // CMP_STRIDE + 1
    n_sel = seq // SEL_BLOCK
    n_sel_pad = -(-n_sel // SUBLANES) * SUBLANES
    mt = _cmp_to_sel_t(n_chunks, n_cmp, n_sel, n_sel_pad)
    gnt = gn.reshape(batch, seq, GATE_PAD).transpose(0, 2, 1)
    out = pl.pallas_call(
        functools.partial(_nsa_prompt_kernel, seq=seq, n_cmp=n_cmp, n_sel=n_sel, n_sel_pad=n_sel_pad),
        grid=(batch, seq // _TQ),
        in_specs=[pl.BlockSpec((None, _TQ, D_MODEL), lambda b, i: (b, i, 0)),
                  pl.BlockSpec((None, GATE_PAD, _TQ), lambda b, i: (b, 0, i)),
                  pl.BlockSpec((None, 2, n_chunks, KV_W), lambda b, i: (b, 0, 0, 0)),
                  pl.BlockSpec((None, seq, 4 * KV_W), lambda b, i: (b, 0, 0)),
                  pl.BlockSpec((2 * KV_W, seq), lambda b, i: (0, b)),
                  pl.BlockSpec((n_sel_pad, n_chunks), lambda b, i: (0, 0))],
        out_specs=pl.BlockSpec((None, D_MODEL, _TQ), lambda b, i: (b, 0, i)),
        out_shape=jax.ShapeDtypeStruct((batch, D_MODEL, seq), BF16),
        scratch_shapes=[pltpu.VMEM((N_KV, seq // _KC, SUBLANES, _TQ), F32)],
        compiler_params=_cparams(("parallel", "arbitrary")),
        name="nsa_prompt",
    )(q.reshape(batch, seq, D_MODEL), gnt, kvc, kvh.reshape(batch, seq, 4 * KV_W), vt, mt)
    return out.transpose(0, 2, 1).reshape(batch * seq, D_MODEL)


def _cmp_to_sel_t(n_chunks, n_cmp, n_sel, n_sel_pad):
    cs = np.arange(n_chunks)[None, :] * CMP_STRIDE
    ss = np.arange(n_sel_pad)[:, None] * SEL_BLOCK
    m = (cs < ss + SEL_BLOCK) & (cs + CMP_BLOCK > ss)
    m = m & (np.arange(n_chunks)[None, :] < n_cmp) & (np.arange(n_sel_pad)[:, None] < n_sel)
    return jnp.asarray(m, F32)


def _block_expand(rows, keys):
    return jnp.asarray(np.arange(keys)[None, :] // SEL_BLOCK == np.arange(rows)[:, None], BF16)


_NEW_PAD = 128


def _slc_page_copy(cache_ref, buf_ref, sem_ref, pt_ref, layer, sample, page, slot, n_pages):
    pid = pt_ref[sample * n_pages + page]
    return pltpu.make_async_copy(cache_ref.at[layer, pid, pl.ds(2 * KV_W, 2 * KV_W), :],
                                 buf_ref.at[slot, :, pl.ds(page * PAGE_SIZE, PAGE_SIZE)], sem_ref.at[slot])


def _joint_softmax(s_a, m_a, s_b, m_b):
    mx = jnp.maximum(jnp.max(jnp.where(m_a, s_a, NEG), axis=-1, keepdims=True),
                     jnp.max(jnp.where(m_b, s_b, NEG), axis=-1, keepdims=True))
    e_a = jnp.where(m_a, jnp.exp(s_a - mx), 0.0)
    e_b = jnp.where(m_b, jnp.exp(s_b - mx), 0.0)
    den = jnp.sum(e_a, axis=-1, keepdims=True) + jnp.sum(e_b, axis=-1, keepdims=True)
    return e_a, e_b, jnp.where(den > 0.0, 1.0 / den, 0.0)


def _nsa_sample_kernel(pt_ref, qbd_ref, gate_ref, kvc_ref, new_ref, win_ref, mt_ref, ex_ref, rr_ref, meta_ref,
                       cache_ref, o_ref, buf_ref, sem_ref, *, layer, n_pages, t_new, n_cmp, n_sel, dec_b):
    smp = pl.program_id(0)
    slot = smp % 2
    past = n_pages * PAGE_SIZE
    rows = N_KV * t_new * Q_PER_KV
    win_rows = win_ref.shape[1]

    def issue(s, sl):
        for page in range(n_pages):
            _slc_page_copy(cache_ref, buf_ref, sem_ref, pt_ref, layer, s, page, sl, n_pages).start()

    @pl.when(smp == 0)
    def _():
        issue(smp, slot)

    @pl.when(smp + 1 < dec_b)
    def _():
        issue(smp + 1, 1 - slot)

    for page in range(n_pages):
        _slc_page_copy(cache_ref, buf_ref, sem_ref, pt_ref, layer, 0, page, slot, n_pages).wait()

    qbd = qbd_ref[...]
    tok = meta_ref[:, 0:1]
    grp = meta_ref[:, 1:2]
    pos = past + tok

    n_chunks = kvc_ref.shape[1]
    n_idx = lax.broadcasted_iota(jnp.int32, (1, n_chunks), 1)
    mask_c = (n_idx * CMP_STRIDE + (CMP_BLOCK - 1) <= pos) & (n_idx < n_cmp)
    p_c = _masked_softmax_rows(_dot_nt(qbd, kvc_ref[0].astype(BF16)), mask_c)
    o_c = _dot(p_c.astype(BF16), kvc_ref[1].astype(BF16))

    imp = jnp.dot(rr_ref[...], jnp.dot(p_c, mt_ref[...], precision=HIGHEST, preferred_element_type=F32),
                  precision=HIGHEST, preferred_element_type=F32)
    blk = lax.broadcasted_iota(jnp.int32, (rows, LANES), 1)
    cur = jnp.right_shift(pos, SEL_SHIFT)
    valid = (blk <= cur) & (blk < n_sel)
    forced = (blk == 0) | (blk == cur) | (blk == cur - 1)
    score = jnp.where(valid, imp + jnp.where(forced, FORCE_BONUS, 0.0), -FORCE_BONUS)
    score = jnp.where(blk < n_sel, score, -jnp.inf)
    sel = _rank_select(score, blk, n_sel, axis=1)
    mask_s = _dot(sel.astype(BF16), ex_ref[...]) > 0.5

    i_new = lax.broadcasted_iota(jnp.int32, (1, _NEW_PAD), 1)
    new_ok = (i_new < t_new) & (i_new <= tok)
    new_blk = past // SEL_BLOCK
    mask_sn = new_ok & (sel[:, new_blk:new_blk + 1] > 0.5)

    ks = buf_ref[slot, 0:KV_W, :].astype(BF16)
    vs = buf_ref[slot, KV_W:2 * KV_W, :].astype(BF16)
    s_s = _dot(qbd, ks)
    s_sn = _dot_nt(qbd, new_ref[:, 0:KV_W])
    e_s, e_sn, inv_s = _joint_softmax(s_s, mask_s, s_sn, mask_sn)
    o_s = (_dot_nt(e_s.astype(BF16), vs) + _dot(e_sn.astype(BF16), new_ref[:, KV_W:2 * KV_W])) * inv_s

    kw = win_ref[0:KV_W, :].astype(BF16)
    vw = win_ref[KV_W:2 * KV_W, :].astype(BF16)
    i_w = lax.broadcasted_iota(jnp.int32, (1, win_rows), 1)
    mask_w = i_w > tok + (win_rows - WINDOW)
    s_w = _dot(qbd, kw)
    s_wn = _dot_nt(qbd, new_ref[:, 2 * KV_W:3 * KV_W])
    e_w, e_wn, inv_w = _joint_softmax(s_w, mask_w, s_wn, new_ok)
    o_w = (_dot_nt(e_w.astype(BF16), vw) + _dot(e_wn.astype(BF16), new_ref[:, 3 * KV_W:4 * KV_W])) * inv_w

    o_all = gate_ref[:, 0:1] * o_c + gate_ref[:, 1:2] * o_s + gate_ref[:, 2:3] * o_w
    out = jnp.zeros((rows, HEAD_DIM), F32)
    for g in range(N_KV):
        out = out + jnp.where(grp == g, o_all[:, g * HEAD_DIM:(g + 1) * HEAD_DIM], 0.0)
    o_ref[...] = out


def _nsa_sample(page_table, qbd, gates, kvc, new_rows, win_t, cache_t, layer, t_new):
    dec_b, n_pages = page_table.shape
    past = n_pages * PAGE_SIZE
    rows = N_KV * t_new * Q_PER_KV
    n_chunks = kvc.shape[3]
    n_cmp = (past + t_new - CMP_BLOCK) // CMP_STRIDE + 1
    n_sel = -(-(past + t_new) // SEL_BLOCK)
    win_rows = win_t.shape[-1]
    mt = _cmp_to_sel_t(n_chunks, n_cmp, n_sel, LANES).T
    ex = _block_expand(LANES, past)
    rr = jnp.asarray(np.arange(rows)[:, None] // Q_PER_KV == np.arange(rows)[None, :] // Q_PER_KV, F32)
    meta_np = np.zeros((rows, LANES), np.int32)
    meta_np[:, 0] = (np.arange(rows) // Q_PER_KV) % t_new
    meta_np[:, 1] = np.arange(rows) // (t_new * Q_PER_KV)
    meta = jnp.asarray(meta_np)
    per = lambda *shape: pl.BlockSpec((None,) + shape, lambda s, pt: (s,) + (0,) * len(shape))
    const = lambda *shape: pl.BlockSpec(shape, lambda s, pt: (0,) * len(shape))
    lper = lambda *shape: pl.BlockSpec((None, None) + shape, lambda s, pt: (layer, s) + (0,) * len(shape))
    grid_spec = pltpu.PrefetchScalarGridSpec(
        num_scalar_prefetch=1,
        grid=(dec_b,),
        in_specs=[per(rows, KV_W), per(rows, LANES), lper(2, n_chunks, KV_W), per(_NEW_PAD, 4 * KV_W),
                  lper(2 * KV_W, win_rows), const(n_chunks, LANES), const(LANES, past), const(rows, rows),
                  const(rows, LANES), pl.BlockSpec(memory_space=pl.ANY)],
        out_specs=per(rows, HEAD_DIM),
        scratch_shapes=[pltpu.VMEM((2, 2 * KV_W, past), F32), pltpu.SemaphoreType.DMA((2,))])
    return pl.pallas_call(
        functools.partial(_nsa_sample_kernel, layer=layer, n_pages=n_pages, t_new=t_new, n_cmp=n_cmp, n_sel=n_sel,
                          dec_b=dec_b),
        grid_spec=grid_spec,
        out_shape=jax.ShapeDtypeStruct((dec_b, rows, HEAD_DIM), F32),
        compiler_params=_cparams(("arbitrary",)),
        name="nsa_sample",
    )(page_table.reshape(-1), qbd, gates, kvc, new_rows, win_t, mt, ex, rr, meta, cache_t)


def _merge_kernel(x_ref, ca_ref, on_ref, gc_ref, gm_ref, wc_ref, wn_ref, wg_ref, wo_ref, g_ref, b_ref, h_ref, *, alpha):
    ya = _dot(ca_ref[...], wc_ref[...])
    yb = _dot(on_ref[...], wn_ref[...])
    yc = _dot(gc_ref[...], wg_ref[...])
    m = (gm_ref[:, 0:D_MODEL] * ya + gm_ref[:, D_MODEL:2 * D_MODEL] * yb + gm_ref[:, 2 * D_MODEL:3 * D_MODEL] * yc)
    mix = _dot(m.astype(BF16), wo_ref[...])
    h_ref[...] = _layer_norm(alpha * x_ref[...] + mix, g_ref[...], b_ref[...])


def _merge(x, ca, on, gc, gm, wc, wn, wg, wo, ln_g, ln_b, alpha, layer):
    n = x.shape[0]
    tm = _pick(n, (512, 256, 128, 64, 32, 16, 8))
    row = lambda w: pl.BlockSpec((tm, w), lambda i: (i, 0))
    wspec = pl.BlockSpec((None, D_MODEL, D_MODEL), lambda i: (layer, 0, 0), pipeline_mode=pl.Buffered(1))
    vec = pl.BlockSpec((1, D_MODEL), lambda i: (0, 0))
    return pl.pallas_call(
        functools.partial(_merge_kernel, alpha=alpha),
        grid=(n // tm,),
        in_specs=[row(D_MODEL), row(D_MODEL), row(D_MODEL), row(D_MODEL), row(3 * D_MODEL),
                  wspec, wspec, wspec, wspec, vec, vec],
        out_specs=row(D_MODEL),
        out_shape=jax.ShapeDtypeStruct((n, D_MODEL), F32),
        compiler_params=_cparams(("parallel",)),
        name="merge",
    )(x, ca, on, gc, gm, wc, wn, wg, wo, ln_g[None], ln_b[None])


_ROUTER_PAD = 128


_MOE_SUB = 256


def _split3_bf16(x):
    hi = x.astype(BF16)
    r1 = x - hi.astype(F32)
    mid = r1.astype(BF16)
    lo = (r1 - mid.astype(F32)).astype(BF16)
    return hi, mid, lo


def _moe_route_kernel(h_ref, wr_ref, br_ref, ltri_ref, ustrict_ref, xs_ref, wts_ref, pt_ref, flag_ref):
    tm = h_ref.shape[0]
    lane = lax.broadcasted_iota(jnp.int32, (1, _ROUTER_PAD), 1)
    h = h_ref[...]
    logit = jnp.dot(h, wr_ref[...], precision=HIGHEST, preferred_element_type=F32) + br_ref[...]
    gl = jnp.where(lane < N_GROUPS, logit, -jnp.inf)
    gmax = jnp.max(gl, axis=-1, keepdims=True)
    g_sel = jnp.min(jnp.where(gl == gmax, lane, _ROUTER_PAD), axis=-1, keepdims=True)
    p_grp = 1.0 / jnp.sum(jnp.exp(gl - gmax), axis=-1, keepdims=True)
    lane_grp = jnp.right_shift(lane - N_GROUPS, EPG_SHIFT)
    in_grp = (lane >= N_GROUPS) & (lane < N_GROUPS + N_EXPERTS) & (lane_grp == g_sel)
    el = jnp.where(in_grp, logit, -jnp.inf)
    v1 = jnp.max(el, axis=-1, keepdims=True)
    i1 = jnp.min(jnp.where(el == v1, lane, _ROUTER_PAD), axis=-1, keepdims=True)
    el2 = jnp.where(lane == i1, -jnp.inf, el)
    v2 = jnp.max(el2, axis=-1, keepdims=True)
    i2 = jnp.min(jnp.where(el2 == v2, lane, _ROUTER_PAD), axis=-1, keepdims=True)
    e2 = jnp.exp(v2 - v1)
    wt = jnp.where(lane == i1, 1.0 / (1.0 + e2), jnp.where(lane == i2, e2 / (1.0 + e2), 0.0)) * p_grp

    onehot = jnp.where(lane == g_sel, 1.0, 0.0)
    cum = _dot(ltri_ref[...], onehot.astype(BF16))
    cnt = jnp.broadcast_to(cum[tm - 1:tm, :], (SUBLANES, _ROUTER_PAD))
    off = jnp.dot(cnt, ustrict_ref[...], precision=HIGHEST, preferred_element_type=F32)[0:1, :]
    dst = jnp.sum(onehot * (off + cum - 1.0), axis=-1, keepdims=True)
    dst_row = jnp.broadcast_to(dst, (tm, LANES)).T[0:1, :]
    s_col = lax.broadcasted_iota(jnp.int32, (tm, 1), 0).astype(F32)
    s_row = lax.broadcasted_iota(jnp.int32, (1, tm), 1).astype(F32)
    p_mat = jnp.where(s_col == dst_row, 1.0, 0.0).astype(BF16)
    pt_ref[...] = jnp.where(s_row == dst, 1.0, 0.0).astype(BF16)

    xs_ref[...] = _dot(p_mat, h.astype(BF16)).astype(BF16)
    hi, mid, lo = _split3_bf16(wt)
    wts = _dot(p_mat, hi) + _dot(p_mat, mid) + _dot(p_mat, lo)
    wts_ref[...] = wts
    rows = []
    for j in range(tm // _MOE_SUB):
        rows.append(jnp.max(wts[j * _MOE_SUB:(j + 1) * _MOE_SUB], axis=0, keepdims=True))
    rows.append(jnp.zeros((SUBLANES - tm // _MOE_SUB, _ROUTER_PAD), F32))
    flag_ref[...] = jnp.where(jnp.concatenate(rows, axis=0) > 0.0, 1, 0).astype(jnp.int32)


def _moe_expert_kernel(flag_ref, h_ref, xs_ref, wts_ref, pt_ref, wg_ref, wu_ref, wd_ref, g_ref, b_ref, o_ref, acc_ref,
                       *, alpha):
    i = pl.program_id(0)
    e = pl.program_id(1)
    tm = h_ref.shape[0]
    lane = lax.broadcasted_iota(jnp.int32, (1, _ROUTER_PAD), 1)

    @pl.when(e == 0)
    def _():
        acc_ref[...] = jnp.zeros_like(acc_ref)

    for j in range(tm // _MOE_SUB):
        @pl.when(flag_ref[(i * SUBLANES + j) * N_EXPERTS + e] != 0)
        def _():
            rows = pl.ds(j * _MOE_SUB, _MOE_SUB)
            x = xs_ref[rows, :]
            wcol = jnp.sum(jnp.where(lane == N_GROUPS + e, wts_ref[rows, :], 0.0), axis=-1, keepdims=True)
            a = _dot(x, wg_ref[...])
            b = _dot(x, wu_ref[...])
            hid = (a * _sigmoid(a)) * b * wcol
            acc_ref[rows, :] += _dot(hid.astype(BF16), wd_ref[...])

    @pl.when(e == N_EXPERTS - 1)
    def _():
        acc = acc_ref[...]
        hi = acc.astype(BF16)
        lo = (acc - hi.astype(F32)).astype(BF16)
        f = _dot(pt_ref[...], hi) + _dot(pt_ref[...], lo)
        o_ref[...] = _layer_norm(alpha * h_ref[...] + f, g_ref[...], b_ref[...])


def _moe(h, wr, br, wg, wu, wd, ln_g, ln_b, alpha, layer):
    n = h.shape[0]
    tm = _pick(n, (1024, 512, 256))
    n_tiles = n // tm
    ltri = jnp.asarray(np.tril(np.ones((tm, tm), np.float32)), BF16)
    ustrict = jnp.asarray(np.triu(np.ones((_ROUTER_PAD, _ROUTER_PAD), np.float32), 1), F32)
    row1 = lambda w: pl.BlockSpec((tm, w), lambda i: (i, 0))
    xs, wts, pt, flags = pl.pallas_call(
        _moe_route_kernel,
        grid=(n_tiles,),
        in_specs=[row1(D_MODEL),
                  pl.BlockSpec((None, D_MODEL, _ROUTER_PAD), lambda i: (layer, 0, 0)),
                  pl.BlockSpec((None, 1, _ROUTER_PAD), lambda i: (layer, 0, 0)),
                  pl.BlockSpec((tm, tm), lambda i: (0, 0)),
                  pl.BlockSpec((_ROUTER_PAD, _ROUTER_PAD), lambda i: (0, 0))],
        out_specs=[row1(D_MODEL), row1(_ROUTER_PAD), pl.BlockSpec((None, tm, tm), lambda i: (i, 0, 0)),
                   pl.BlockSpec((None, SUBLANES, _ROUTER_PAD), lambda i: (i, 0, 0))],
        out_shape=[jax.ShapeDtypeStruct((n, D_MODEL), BF16), jax.ShapeDtypeStruct((n, _ROUTER_PAD), F32),
                   jax.ShapeDtypeStruct((n_tiles, tm, tm), BF16),
                   jax.ShapeDtypeStruct((n_tiles, SUBLANES, _ROUTER_PAD), jnp.int32)],
        compiler_params=_cparams(("parallel",)),
        name="moe_route",
    )(h, wr, br, ltri, ustrict)
    flags = flags[:, :, N_GROUPS:N_GROUPS + N_EXPERTS].reshape(-1)
    row = lambda w: pl.BlockSpec((tm, w), lambda i, e, fl: (i, 0))
    vec = pl.BlockSpec((1, D_MODEL), lambda i, e, fl: (0, 0))
    grid_spec = pltpu.PrefetchScalarGridSpec(
        num_scalar_prefetch=1,
        grid=(n_tiles, N_EXPERTS),
        in_specs=[row(D_MODEL), row(D_MODEL), row(_ROUTER_PAD),
                  pl.BlockSpec((None, tm, tm), lambda i, e, fl: (i, 0, 0)),
                  pl.BlockSpec((None, None, D_MODEL, EXPERT_HIDDEN), lambda i, e, fl: (layer, e, 0, 0)),
                  pl.BlockSpec((None, None, D_MODEL, EXPERT_HIDDEN), lambda i, e, fl: (layer, e, 0, 0)),
                  pl.BlockSpec((None, None, EXPERT_HIDDEN, D_MODEL), lambda i, e, fl: (layer, e, 0, 0)),
                  vec, vec],
        out_specs=row(D_MODEL),
        scratch_shapes=[pltpu.VMEM((tm, D_MODEL), F32)])
    return pl.pallas_call(
        functools.partial(_moe_expert_kernel, alpha=alpha),
        grid_spec=grid_spec,
        out_shape=jax.ShapeDtypeStruct((n, D_MODEL), F32),
        compiler_params=_cparams(("parallel", "arbitrary")),
        name="moe_experts",
    )(flags, h, xs, wts, pt, wg, wu, wd, ln_g[None], ln_b[None])


def _prep_in_proj(w_in, b_in):
    c_gn = 2 * D_MODEL + D_MODEL + KV_COLS
    n_gn = 3 * N_HEADS
    pad = GATE_PAD - n_gn

    def rearr(a):
        z = jnp.zeros(a.shape[:-1] + (pad,), a.dtype)
        return jnp.concatenate([a[..., :c_gn], a[..., c_gn + n_gn:], a[..., c_gn:c_gn + n_gn], z], axis=-1)

    return rearr(w_in).astype(BF16), rearr(b_in)[:, None, :]


def _prep_cmp(cmp_pe, cmp_w):
    depth = cmp_w.shape[0]
    w = cmp_w.reshape(depth, 2, CMP_SUB, CMP_STRIDE, HEAD_DIM, HEAD_DIM)
    eye = jnp.eye(N_KV, dtype=cmp_w.dtype)
    wbd = jnp.einsum('zshlde,gk->zslgdhke', w, eye).reshape(depth, 2, _CMP_K, _CMP_N).astype(BF16)
    pe = cmp_pe.reshape(depth, 2, CMP_SUB, CMP_STRIDE, 1, HEAD_DIM)
    pe = jnp.broadcast_to(pe, (depth, 2, CMP_SUB, CMP_STRIDE, N_KV, HEAD_DIM)).reshape(depth, 2, CMP_SUB, _CMP_K)
    pe8 = jnp.concatenate([pe, jnp.zeros((depth, 2, SUBLANES - CMP_SUB, _CMP_K), pe.dtype)], axis=2)
    return pe8, wbd


def _deinterleave_perm():
    cpp = PAGE_SIZE // CMP_STRIDE
    r = np.arange(PAGE_SIZE)
    src = (r % cpp) * CMP_STRIDE + r // cpp
    return jnp.asarray(src[:, None] == np.arange(PAGE_SIZE)[None, :], BF16)


def kernel(x_prompt, x_sample, cache_kv, state_win, state_conv, page_table, w_in, b_in, conv_w, conv_b, conv_ln_g, conv_ln_b, w_up_conv, cmp_pe, cmp_w, w_up_nsa, gm_ln_g, gm_ln_b, gm_ws, gm_bs, w_up_gm, w_o, ln_g, ln_b, router_g_w, router_g_b, router_e_w, router_e_b, moe_w_gate, moe_w_up, moe_w_down):
    depth = w_in.shape[0]
    batch, seq, _ = x_prompt.shape
    dec_b, t_new, _ = x_sample.shape
    n_pages = page_table.shape[1]
    past = n_pages * PAGE_SIZE
    win_buf = state_win.shape[2]
    alpha = float((2 * depth) ** 0.25)
    assert seq % _KC == 0 and seq % GM_CHUNK == 0 and seq >= WINDOW
    assert t_new < CMP_STRIDE and t_new <= _NEW_PAD and past % SEL_BLOCK == 0 and win_buf == min(WINDOW, past)

    w_in_r, b_in_r = _prep_in_proj(w_in, b_in)
    c_kv = 2 * D_MODEL + D_MODEL
    v_slc, v_win = slice(c_kv + 3 * KV_W, c_kv + 4 * KV_W), slice(c_kv + 5 * KV_W, c_kv + 6 * KV_W)
    wvt = jnp.concatenate([w_in[:, :, v_slc], w_in[:, :, v_win]], axis=2).astype(BF16).transpose(0, 2, 1)
    bvt = jnp.concatenate([b_in[:, v_slc], b_in[:, v_win]], axis=1)
    bvt = jnp.broadcast_to(bvt[:, :, None], (depth, 2 * KV_W, LANES))
    pe8, wbd = _prep_cmp(cmp_pe, cmp_w)
    wc16, wn16, wg16, wo16 = (w.astype(BF16) for w in (w_up_conv, w_up_nsa, w_up_gm, w_o))
    mg16, mu16, md16 = moe_w_gate.astype(BF16), moe_w_up.astype(BF16), moe_w_down.astype(BF16)
    zr = jnp.zeros((depth, D_MODEL, _ROUTER_PAD - N_GROUPS - N_EXPERTS), F32)
    w_router = jnp.concatenate([router_g_w, router_e_w.reshape(depth, D_MODEL, N_EXPERTS), zr], axis=-1)
    zb = jnp.zeros((depth, _ROUTER_PAD - N_GROUPS - N_EXPERTS), F32)
    b_router = jnp.concatenate([router_g_b, router_e_b.reshape(depth, N_EXPERTS), zb], axis=-1)[:, None, :]
    gw = D_MODEL // GM_GROUPS
    wrow = jnp.repeat(gm_ws[:, :, :t_new, :t_new].transpose(0, 2, 3, 1).reshape(depth, t_new * t_new, GM_GROUPS), gw, axis=-1)
    bsrow = jnp.repeat(gm_bs[:, :, :t_new].transpose(0, 2, 1), gw, axis=-1)

    cache_t = cache_kv.transpose(0, 1, 3, 4, 5, 2).reshape(depth, cache_kv.shape[1], N_KV_SLOTS * KV_W, PAGE_SIZE)
    win_t_all = state_win.transpose(0, 1, 3, 4, 5, 2).reshape(depth, dec_b, 2 * KV_W, win_buf)

    kvc_s_all = _cmp_sample(page_table, cache_t, _deinterleave_perm(), pe8, wbd)

    xp = x_prompt.reshape(batch * seq, D_MODEL)
    xs = x_sample.reshape(dec_b * t_new, D_MODEL)
    kvp, kvs, wnp, wns, cvp, cvs, gmv = [], [], [], [], [], [], []
    wb = min(WINDOW, seq)
    eye_g = jnp.eye(N_KV, dtype=BF16)
    for l in range(depth):
        glu, q, kv4, kwn, kvh, u, v, gm, gn, vt = _in_proj(xp, w_in_r, b_in_r, wvt, bvt, l)
        ca = _conv_prompt(glu, conv_w[l], conv_b[l], conv_ln_g[l], conv_ln_b[l], batch, seq)
        gc = _gmlp_prompt(u, v, gm_ln_g[l], gm_ln_b[l], gm_ws[l], gm_bs[l])
        kvc = _cmp_prompt(kv4, pe8, wbd, batch, seq, l)
        on = _nsa_prompt(q, gn, kvc, kvh, vt, batch, seq)
        h = _merge(xp, ca, on, gc, gm, wc16, wn16, wg16, wo16, ln_g[l, 0], ln_b[l, 0], alpha, l)
        xp = _moe(h, w_router, b_router, mg16, mu16, md16, ln_g[l, 1], ln_b[l, 1], alpha, l)
        kvp.append(kv4.reshape(batch, seq, N_KV_SLOTS, N_KV, HEAD_DIM))
        wnp.append(kwn.reshape(batch, seq, 2 * KV_W)[:, seq - wb:].reshape(batch, wb, 2, N_KV, HEAD_DIM))
        cvp.append(glu.reshape(batch, seq, D_MODEL)[:, seq - (CONV_WIDTH - 1):])

        glu, q, kv4, kwn, kvh, u, v, gm, gn, _ = _in_proj(xs, w_in_r, b_in_r, wvt, bvt, l)
        conv_ctx = jnp.concatenate([state_conv[l], glu.reshape(dec_b, t_new, D_MODEL)], axis=1)
        tmaj = lambda a: a.reshape(dec_b, t_new, D_MODEL).transpose(1, 0, 2)
        ca_t, gc_t, vn_t = _sample_seq(conv_ctx.transpose(1, 0, 2), tmaj(u), tmaj(v), conv_w[l], conv_b[l],
                                       conv_ln_g[l], conv_ln_b[l], gm_ln_g[l], gm_ln_b[l], wrow[l], bsrow[l])
        bmaj = lambda a: a.transpose(1, 0, 2).reshape(dec_b * t_new, D_MODEL)
        q5 = q.reshape(dec_b, t_new, N_KV, Q_PER_KV, HEAD_DIM).transpose(0, 2, 1, 3, 4)
        qbd = jnp.einsum('bgtrd,gk->bgtrkd', q5, eye_g).reshape(dec_b, N_KV * t_new * Q_PER_KV, KV_W)
        g5 = gn[:, :3 * N_HEADS].reshape(dec_b, t_new, N_KV, Q_PER_KV, 3).transpose(0, 2, 1, 3, 4)
        gates = jnp.pad(g5.reshape(dec_b, N_KV * t_new * Q_PER_KV, 3), ((0, 0), (0, 0), (0, LANES - 3)))
        new_rows = jnp.pad(kvh.reshape(dec_b, t_new, 4 * KV_W), ((0, 0), (0, _NEW_PAD - t_new), (0, 0)))
        o5 = _nsa_sample(page_table, qbd, gates, kvc_s_all, new_rows, win_t_all, cache_t, l, t_new)
        on = o5.reshape(dec_b, N_KV, t_new, Q_PER_KV, HEAD_DIM).transpose(0, 2, 1, 3, 4)
        on = on.reshape(dec_b * t_new, D_MODEL).astype(BF16)
        h = _merge(xs, bmaj(ca_t), on, bmaj(gc_t), gm, wc16, wn16, wg16, wo16, ln_g[l, 0], ln_b[l, 0], alpha, l)
        xs = _moe(h, w_router, b_router, mg16, mu16, md16, ln_g[l, 1], ln_b[l, 1], alpha, l)
        kvs.append(kv4.reshape(dec_b, t_new, N_KV_SLOTS, N_KV, HEAD_DIM))
        wns.append(kwn.reshape(dec_b, t_new, 2, N_KV, HEAD_DIM))
        cvs.append(conv_ctx[:, t_new:])
        gmv.append(vn_t.transpose(1, 0, 2))

    no_pad = (0, 0, 0)
    win_sample = lax.pad(state_win, jnp.zeros((), state_win.dtype), [no_pad, no_pad, (-t_new, t_new, 0), no_pad, no_pad, no_pad])
    win_sample = lax.dynamic_update_slice(win_sample, jnp.stack(wns), (0, 0, win_buf - t_new, 0, 0, 0))
    return (xp.reshape(batch, seq, D_MODEL), xs.reshape(dec_b, t_new, D_MODEL), jnp.stack(kvp), jnp.stack(kvs),
            jnp.stack(wnp), win_sample, jnp.stack(cvp), jnp.stack(cvs), jnp.stack(gmv))
```

```python
import functools

import jax
import jax.numpy as jnp
import numpy as np
from jax import lax
from jax.experimental import pallas as pl
from jax.experimental.pallas import tpu as pltpu

F32 = jnp.float32
BF16 = jnp.bfloat16

D_MODEL = 1024
CONV_WIDTH = 31
N_HEADS = 16
N_KV = 4
HEAD_DIM = 64
Q_PER_KV = N_HEADS // N_KV
CMP_BLOCK = 32
CMP_STRIDE = 16
CMP_SUB = CMP_BLOCK // CMP_STRIDE
SEL_BLOCK = 64
SEL_SHIFT = 6
SEL_TOPN = 8
WINDOW = 512
PAGE_SIZE = 128
N_KV_SLOTS = 4
FORCE_BONUS = 1.0e4
GM_GROUPS = 4
GM_CHUNK = 128
N_GROUPS = 4
EXPERTS_PER_GROUP = 4
EPG_SHIFT = 2
N_EXPERTS = N_GROUPS * EXPERTS_PER_GROUP
EXPERT_HIDDEN = 512
LN_EPS = 1e-5
NEG = -1e30
KV_W = N_KV * HEAD_DIM
KV_COLS = 6 * KV_W
GATE_PAD = 128

V7X_VMEM_BYTES = 64 * 1024 * 1024
VMEM_LIMIT = V7X_VMEM_BYTES * 7 // 8
LANES = 128
SUBLANES = 8

HIGHEST = lax.Precision.HIGHEST


def _cparams(sem):
    return pltpu.CompilerParams(dimension_semantics=sem, vmem_limit_bytes=VMEM_LIMIT)


def _pick(n, cands):
    for c in cands:
        if n % c == 0:
            return c
    raise ValueError(f"no tile in {cands} divides {n}")


def _sigmoid(x):
    return 1.0 / (1.0 + jnp.exp(-x))


def _gelu_tanh(x):
    return 0.5 * x * (1.0 + jnp.tanh(np.sqrt(2.0 / np.pi).astype(np.float32) * (x + 0.044715 * (x * x * x))))


def _layer_norm(x, g, b):
    mu = jnp.mean(x, axis=-1, keepdims=True)
    xc = x - mu
    var = jnp.mean(xc * xc, axis=-1, keepdims=True)
    return xc * lax.rsqrt(var + LN_EPS) * g + b


def _dot(a, b):
    return jnp.dot(a, b, preferred_element_type=F32)


def _dot_nt(a, b, precision=None):
    return lax.dot_general(a, b, (((1,), (1,)), ((), ())), preferred_element_type=F32, precision=precision)


def _dot_tn(a, b):
    return lax.dot_general(a, b, (((0,), (0,)), ((), ())), preferred_element_type=F32)


def _masked_softmax_rows(s, mask):
    s = jnp.where(mask, s, NEG)
    m = jnp.max(s, axis=-1, keepdims=True)
    e = jnp.exp(s - m)
    p = e / jnp.sum(e, axis=-1, keepdims=True)
    return jnp.where(mask, p, 0.0)


_C_A = 0
_C_GATE = _C_A + D_MODEL
_C_Q = _C_GATE + D_MODEL
_C_KV = _C_Q + D_MODEL
_C_U = _C_KV + KV_COLS
_C_V = _C_U + D_MODEL
_C_GM = _C_V + D_MODEL
_C_GN = _C_GM + 3 * D_MODEL
IN_COLS = _C_GN + GATE_PAD


def _in_proj_kernel(x_ref, w_ref, b_ref, wvt_ref, bvt_ref, glu_ref, q_ref, kv4_ref, kwn_ref, kvh_ref, u_ref, v_ref, gm_ref,
                    gn_ref, vt_ref):
    x = x_ref[...].astype(BF16)

    def seg(lo, hi):
        return _dot(x, w_ref[:, lo:hi]) + b_ref[:, lo:hi]

    glu_ref[...] = seg(_C_A, _C_GATE) * _sigmoid(seg(_C_GATE, _C_Q))
    q_ref[...] = (seg(_C_Q, _C_KV) * (HEAD_DIM ** -0.5)).astype(BF16)
    kv = seg(_C_KV, _C_U)
    kv4_ref[...] = kv[:, :N_KV_SLOTS * KV_W]
    kwn_ref[...] = kv[:, N_KV_SLOTS * KV_W:]
    kvh_ref[...] = kv[:, 2 * KV_W:].astype(BF16)
    u_ref[...] = _gelu_tanh(seg(_C_U, _C_V)).astype(BF16)
    v_ref[...] = _gelu_tanh(seg(_C_V, _C_GM))
    gm_ref[...] = _sigmoid(seg(_C_GM, _C_GN))
    gn_ref[...] = _sigmoid(seg(_C_GN, IN_COLS))
    vt_ref[...] = (_dot_nt(wvt_ref[...], x) + bvt_ref[:, 0:1]).astype(BF16)


def _in_proj(x, w_r, b_r, wvt, bvt, layer):
    n = x.shape[0]
    tm = _pick(n, (256, 128, 64, 32, 16, 8))
    row = lambda w: pl.BlockSpec((tm, w), lambda i: (i, 0))
    return pl.pallas_call(
        _in_proj_kernel,
        grid=(n // tm,),
        in_specs=[row(D_MODEL),
                  pl.BlockSpec((None, D_MODEL, IN_COLS), lambda i: (layer, 0, 0), pipeline_mode=pl.Buffered(1)),
                  pl.BlockSpec((None, 1, IN_COLS), lambda i: (layer, 0, 0)),
                  pl.BlockSpec((None, 2 * KV_W, D_MODEL), lambda i: (layer, 0, 0)),
                  pl.BlockSpec((None, 2 * KV_W, LANES), lambda i: (layer, 0, 0))],
        out_specs=[row(D_MODEL), row(D_MODEL), row(N_KV_SLOTS * KV_W), row(2 * KV_W), row(4 * KV_W), row(D_MODEL),
                   row(D_MODEL), row(3 * D_MODEL), row(GATE_PAD),
                   pl.BlockSpec((2 * KV_W, tm), lambda i: (0, i))],
        out_shape=[jax.ShapeDtypeStruct((n, D_MODEL), F32),
                   jax.ShapeDtypeStruct((n, D_MODEL), BF16),
                   jax.ShapeDtypeStruct((n, N_KV_SLOTS * KV_W), F32),
                   jax.ShapeDtypeStruct((n, 2 * KV_W), F32),
                   jax.ShapeDtypeStruct((n, 4 * KV_W), BF16),
                   jax.ShapeDtypeStruct((n, D_MODEL), BF16),
                   jax.ShapeDtypeStruct((n, D_MODEL), F32),
                   jax.ShapeDtypeStruct((n, 3 * D_MODEL), F32),
                   jax.ShapeDtypeStruct((n, GATE_PAD), F32),
                   jax.ShapeDtypeStruct((2 * KV_W, n), BF16)],
        compiler_params=_cparams(("parallel",)),
        name="in_proj",
    )(x, w_r, b_r, wvt, bvt)


_CONV_HALO = 32
_CONV_ROWS = SUBLANES
_CONV_LEAD = _CONV_HALO - (CONV_WIDTH - 1)
_CONV_SHIFT_EXTRA = 24


def _conv_prompt_kernel(halo_ref, glu_ref, w_ref, cb_ref, g_ref, b_ref, out_ref, win_ref, sh_ref, y_ref, *, tq):
    i = pl.program_id(1)
    win_ref[0:_CONV_HALO, :] = jnp.where(i > 0, halo_ref[...], 0.0)
    win_ref[_CONV_HALO:_CONV_HALO + tq, :] = glu_ref[...]
    n_sh = tq + _CONV_SHIFT_EXTRA
    for p in range(1, SUBLANES):
        sh_ref[p, 0:n_sh, :] = win_ref[p:p + n_sh, :]

    def chunk(c, carry):
        r0 = pl.multiple_of(c * _CONV_ROWS, _CONV_ROWS)
        acc = jnp.zeros((_CONV_ROWS, D_MODEL), F32)
        for k in range(CONV_WIDTH):
            phase, base = (_CONV_LEAD + k) % SUBLANES, (_CONV_LEAD + k) // SUBLANES * SUBLANES
            if phase == 0:
                x = win_ref[pl.ds(r0 + base, _CONV_ROWS), :]
            else:
                x = sh_ref[phase, pl.ds(r0 + base, _CONV_ROWS), :]
            acc = acc + x * w_ref[k]
        y_ref[pl.ds(r0, _CONV_ROWS), :] = acc
        return carry

    lax.fori_loop(0, tq // _CONV_ROWS, chunk, 0)
    y = _layer_norm(y_ref[...] + cb_ref[...], g_ref[...], b_ref[...])
    out_ref[...] = (y * _sigmoid(y)).astype(BF16)


def _conv_prompt(glu, conv_w, conv_b, ln_g, ln_b, batch, seq):
    tq = _pick(seq, (256, 128))
    glu3 = glu.reshape(batch, seq, D_MODEL)
    hb = tq // _CONV_HALO
    vec = pl.BlockSpec((1, D_MODEL), lambda b, i: (0, 0))
    out = pl.pallas_call(
        functools.partial(_conv_prompt_kernel, tq=tq),
        grid=(batch, seq // tq),
        in_specs=[pl.BlockSpec((None, _CONV_HALO, D_MODEL), lambda b, i: (b, jnp.maximum(i * hb - 1, 0), 0)),
                  pl.BlockSpec((None, tq, D_MODEL), lambda b, i: (b, i, 0)),
                  pl.BlockSpec((CONV_WIDTH, SUBLANES, D_MODEL), lambda b, i: (0, 0, 0)),
                  vec, vec, vec],
        out_specs=pl.BlockSpec((None, tq, D_MODEL), lambda b, i: (b, i, 0)),
        out_shape=jax.ShapeDtypeStruct((batch, seq, D_MODEL), BF16),
        scratch_shapes=[pltpu.VMEM((_CONV_HALO + tq, D_MODEL), F32),
                        pltpu.VMEM((SUBLANES, tq + _CONV_SHIFT_EXTRA, D_MODEL), F32),
                        pltpu.VMEM((tq, D_MODEL), F32)],
        compiler_params=_cparams(("parallel", "parallel")),
        name="conv_prompt",
    )(glu3, glu3, jnp.broadcast_to(conv_w[:, None, :], (CONV_WIDTH, SUBLANES, D_MODEL)), conv_b[None], ln_g[None],
      ln_b[None])
    return out.reshape(batch * seq, D_MODEL)


def _gmlp_prompt_kernel(u_ref, v_ref, g_ref, b_ref, ws_ref, bs_ref, out_ref):
    vn = _layer_norm(v_ref[...], g_ref[...], b_ref[...]).astype(BF16)
    ri = lax.broadcasted_iota(jnp.int32, (GM_CHUNK, GM_CHUNK), 0)
    ci = lax.broadcasted_iota(jnp.int32, (GM_CHUNK, GM_CHUNK), 1)
    gw = D_MODEL // GM_GROUPS
    for g in range(GM_GROUPS):
        w = jnp.where(ri >= ci, ws_ref[g], 0.0).astype(BF16)
        mixed = _dot(w, vn[:, g * gw:(g + 1) * gw]) + bs_ref[:, g:g + 1]
        out_ref[:, g * gw:(g + 1) * gw] = (u_ref[:, g * gw:(g + 1) * gw].astype(F32) * mixed).astype(BF16)


def _gmlp_prompt(u, v, ln_g, ln_b, gm_ws, gm_bs):
    n = u.shape[0]
    vec = pl.BlockSpec((1, D_MODEL), lambda i: (0, 0))
    row = pl.BlockSpec((GM_CHUNK, D_MODEL), lambda i: (i, 0))
    return pl.pallas_call(
        _gmlp_prompt_kernel,
        grid=(n // GM_CHUNK,),
        in_specs=[row, row, vec, vec,
                  pl.BlockSpec((GM_GROUPS, GM_CHUNK, GM_CHUNK), lambda i: (0, 0, 0)),
                  pl.BlockSpec((GM_CHUNK, GM_GROUPS), lambda i: (0, 0))],
        out_specs=row,
        out_shape=jax.ShapeDtypeStruct((n, D_MODEL), BF16),
        compiler_params=_cparams(("parallel",)),
        name="gmlp_prompt",
    )(u, v, ln_g[None], ln_b[None], gm_ws, gm_bs.T)


def _sample_seq_kernel(ctx_ref, u_ref, v_ref, cw_ref, cb_ref, cg_ref, cbb_ref, gg_ref, gb_ref, wrow_ref, bsrow_ref,
                       ca_ref, gc_ref, vn_ref, *, t_new):
    for t in range(t_new):
        acc = ctx_ref[t] * cw_ref[0:1, :]
        for k in range(1, CONV_WIDTH):
            acc = acc + ctx_ref[t + k] * cw_ref[k:k + 1, :]
        y = _layer_norm(acc + cb_ref[...], cg_ref[...], cbb_ref[...])
        ca_ref[t] = (y * _sigmoid(y)).astype(BF16)
    for t in range(t_new):
        vn_ref[t] = _layer_norm(v_ref[t], gg_ref[...], gb_ref[...])
    for i in range(t_new):
        mixed = bsrow_ref[i:i + 1, :]
        for j in range(i + 1):
            mixed = mixed + wrow_ref[i * t_new + j:i * t_new + j + 1, :] * vn_ref[j]
        gc_ref[i] = (u_ref[i].astype(F32) * mixed).astype(BF16)


def _sample_seq(ctx_t, u_t, v_t, conv_w, conv_b, cg, cb, gg, gb, wrow, bsrow):
    rows, dec_b, _ = ctx_t.shape
    t_new = u_t.shape[0]
    sb = _pick(dec_b, (32, 16, 8))
    vec = pl.BlockSpec((1, D_MODEL), lambda i: (0, 0))
    blk = lambda r: pl.BlockSpec((r, sb, D_MODEL), lambda i: (0, i, 0))
    full = lambda r: pl.BlockSpec((r, D_MODEL), lambda i: (0, 0))
    return pl.pallas_call(
        functools.partial(_sample_seq_kernel, t_new=t_new),
        grid=(dec_b // sb,),
        in_specs=[blk(rows), blk(t_new), blk(t_new), full(CONV_WIDTH), vec, vec, vec, vec, vec,
                  full(t_new * t_new), full(t_new)],
        out_specs=[blk(t_new), blk(t_new), blk(t_new)],
        out_shape=[jax.ShapeDtypeStruct((t_new, dec_b, D_MODEL), BF16),
                   jax.ShapeDtypeStruct((t_new, dec_b, D_MODEL), BF16),
                   jax.ShapeDtypeStruct((t_new, dec_b, D_MODEL), F32)],
        compiler_params=_cparams(("parallel",)),
        name="sample_seq",
    )(ctx_t, u_t, v_t, conv_w, conv_b[None], cg[None], cb[None], gg[None], gb[None], wrow, bsrow)


_CMP_K = CMP_STRIDE * KV_W
_CMP_N = CMP_SUB * KV_W


def _combine_halves(y, bias):
    first = y[:, :KV_W] + bias[0:1, :KV_W]
    second = y[:, KV_W:] + bias[1:2, KV_W:]
    return first + pltpu.roll(second, shift=y.shape[0] - 1, axis=0)


def _cmp_prompt_kernel(lo_ref, hi_ref, pe_ref, w_ref, out_ref, *, n_chunks):
    bias = _dot(pe_ref[...].astype(BF16), w_ref[...])
    y = jnp.zeros((n_chunks, _CMP_N), F32)
    for l in range(CMP_STRIDE):
        rows = jnp.concatenate([lo_ref[pl.ds(l, n_chunks, stride=CMP_STRIDE), :],
                                hi_ref[pl.ds(l, n_chunks, stride=CMP_STRIDE), :]], axis=-1).astype(BF16)
        y = y + _dot(rows, w_ref[l * KV_W:(l + 1) * KV_W, :])
    out_ref[...] = _combine_halves(y, bias)


def _cmp_prompt(kv4, pe8, wbd, batch, seq, layer):
    n_chunks = seq // CMP_STRIDE
    kv3 = kv4.reshape(batch, seq, N_KV_SLOTS * KV_W)
    return pl.pallas_call(
        functools.partial(_cmp_prompt_kernel, n_chunks=n_chunks),
        grid=(batch, 2),
        in_specs=[pl.BlockSpec((None, seq, LANES), lambda b, s: (b, 0, 2 * s)),
                  pl.BlockSpec((None, seq, LANES), lambda b, s: (b, 0, 2 * s + 1)),
                  pl.BlockSpec((None, None, SUBLANES, _CMP_K), lambda b, s: (layer, s, 0, 0)),
                  pl.BlockSpec((None, None, _CMP_K, _CMP_N), lambda b, s: (layer, s, 0, 0))],
        out_specs=pl.BlockSpec((None, None, n_chunks, KV_W), lambda b, s: (b, s, 0, 0)),
        out_shape=jax.ShapeDtypeStruct((batch, 2, n_chunks, KV_W), F32),
        compiler_params=_cparams(("parallel", "parallel")),
        name="cmp_prompt",
    )(kv3, kv3, pe8, wbd)


def _cmp_page_copy(cache_ref, buf_ref, sem_ref, pt_ref, layer, sample, page, slot, k, n_pages):
    pid = pt_ref[sample * n_pages + page]
    return pltpu.make_async_copy(cache_ref.at[layer, pid, pl.ds(0, 2 * KV_W), :], buf_ref.at[slot, k], sem_ref.at[slot])


def _cmp_sample_kernel(pt_ref, perm_ref, pe_ref, w_ref, cache_ref, out_ref, buf_ref, rk_ref, rv_ref, sem_ref,
                       *, sb, n_pages, steps_per_layer, n_steps):
    step = pl.program_id(0)
    slot = step % 2
    per_step = sb * n_pages

    def issue(st, sl):
        layer = st // steps_per_layer
        s0 = (st % steps_per_layer) * sb

        def body(k, carry):
            _cmp_page_copy(cache_ref, buf_ref, sem_ref, pt_ref, layer, s0 + k // n_pages, k % n_pages, sl, k,
                           n_pages).start()
            return carry

        lax.fori_loop(0, per_step, body, 0)

    @pl.when(step == 0)
    def _():
        issue(step, slot)

    @pl.when(step + 1 < n_steps)
    def _():
        issue(step + 1, 1 - slot)

    def wait_body(k, carry):
        _cmp_page_copy(cache_ref, buf_ref, sem_ref, pt_ref, 0, 0, 0, slot, k, n_pages).wait()
        return carry

    lax.fori_loop(0, per_step, wait_body, 0)

    chunks_per_page = PAGE_SIZE // CMP_STRIDE

    def page_body(k, carry):
        a = buf_ref[slot, k].astype(BF16)
        t = _dot_nt(perm_ref[...], a)
        r0 = pl.multiple_of(k * chunks_per_page, chunks_per_page)
        for l in range(CMP_STRIDE):
            piece = t[l * chunks_per_page:(l + 1) * chunks_per_page, :]
            rk_ref[pl.ds(r0, chunks_per_page), l * KV_W:(l + 1) * KV_W] = piece[:, :KV_W]
            rv_ref[pl.ds(r0, chunks_per_page), l * KV_W:(l + 1) * KV_W] = piece[:, KV_W:]
        return carry

    lax.fori_loop(0, per_step, page_body, 0, unroll=_pick(per_step, (4, 2, 1)))

    n_chunks = n_pages * chunks_per_page
    for s, r_ref in ((0, rk_ref), (1, rv_ref)):
        bias = _dot(pe_ref[s].astype(BF16), w_ref[s])
        y = _dot(r_ref[...].astype(BF16), w_ref[s])
        blocks = _combine_halves(y, bias)
        for j in range(sb):
            out_ref[j, s] = blocks[j * n_chunks:(j + 1) * n_chunks]


def _cmp_sample(page_table, cache_t, perm, pe8, wbd):
    depth = cache_t.shape[0]
    dec_b, n_pages = page_table.shape
    sb = _pick(dec_b, (2, 1))
    steps_per_layer = dec_b // sb
    n_steps = depth * steps_per_layer
    n_chunks = n_pages * (PAGE_SIZE // CMP_STRIDE)
    rows = sb * n_chunks
    grid_spec = pltpu.PrefetchScalarGridSpec(
        num_scalar_prefetch=1,
        grid=(n_steps,),
        in_specs=[pl.BlockSpec((PAGE_SIZE, PAGE_SIZE), lambda i, pt: (0, 0)),
                  pl.BlockSpec((None, 2, SUBLANES, _CMP_K), lambda i, pt: (i // steps_per_layer, 0, 0, 0)),
                  pl.BlockSpec((None, 2, _CMP_K, _CMP_N), lambda i, pt: (i // steps_per_layer, 0, 0, 0)),
                  pl.BlockSpec(memory_space=pl.ANY)],
        out_specs=pl.BlockSpec((sb, 2, n_chunks, KV_W), lambda i, pt: (i, 0, 0, 0)),
        scratch_shapes=[pltpu.VMEM((2, sb * n_pages, 2 * KV_W, PAGE_SIZE), F32),
                        pltpu.VMEM((rows, _CMP_K), F32),
                        pltpu.VMEM((rows, _CMP_K), F32),
                        pltpu.SemaphoreType.DMA((2,))])
    out = pl.pallas_call(
        functools.partial(_cmp_sample_kernel, sb=sb, n_pages=n_pages, steps_per_layer=steps_per_layer,
                          n_steps=n_steps),
        grid_spec=grid_spec,
        out_shape=jax.ShapeDtypeStruct((depth * dec_b, 2, n_chunks, KV_W), F32),
        compiler_params=_cparams(("arbitrary",)),
        name="cmp_sample",
    )(page_table.reshape(-1), perm, pe8, wbd, cache_t)
    return out.reshape(depth, dec_b, 2, n_chunks, KV_W)


_TQ = 256
_KC = 256


def _rank_select(score, blk, n_blocks, axis):
    rank = jnp.zeros(score.shape, F32)
    for sp in range(n_blocks):
        row = lax.slice_in_dim(score, sp, sp + 1, axis=axis)
        before = jnp.where(blk > sp, 1.0, 0.0)
        rank = rank + jnp.where(row > score, 1.0, jnp.where(row == score, before, 0.0))
    return jnp.where(rank < float(min(SEL_TOPN, n_blocks)), 1.0, 0.0)


def _nsa_prompt_kernel(q_ref, gnt_ref, kvc_ref, kh_ref, vt_ref, mt_ref, o_ref, bias_ref, *, seq, n_cmp, n_sel, n_sel_pad):
    i = pl.program_id(1)
    t0 = i * _TQ
    cols = Q_PER_KV * _TQ
    n_chunks = kvc_ref.shape[1]
    win_keys = min(WINDOW + _TQ, seq)
    bpc = _KC // SEL_BLOCK

    pos = t0 + (lax.broadcasted_iota(jnp.int32, (1, cols), 1) & (_TQ - 1))
    pos_t = t0 + lax.broadcasted_iota(jnp.int32, (1, _TQ), 1)

    n_idx = lax.broadcasted_iota(jnp.int32, (n_chunks, 1), 0)
    mask_c = (n_idx * CMP_STRIDE + (CMP_BLOCK - 1) <= pos) & (n_idx < n_cmp)

    blk = lax.broadcasted_iota(jnp.int32, (n_sel_pad, _TQ), 0)
    cur = jnp.right_shift(pos_t, SEL_SHIFT)
    valid = (blk <= cur) & (blk < n_sel)
    forced = (blk == 0) | (blk == cur) | (blk == cur - 1)

    w_start = pl.multiple_of(jnp.maximum(t0 + _TQ - win_keys, 0), _TQ)
    wpos = w_start + lax.broadcasted_iota(jnp.int32, (win_keys, 1), 0)
    bias_w = jnp.where((wpos <= pos) & (wpos > pos - WINDOW), 0.0, NEG)
    n_kc = (t0 + _TQ + _KC - 1) // _KC
    kidx = lax.broadcasted_iota(jnp.int32, (_KC, 1), 0)

    qgs, o_cs = [], []
    for g in range(N_KV):
        lo = g * HEAD_DIM
        qg = jnp.concatenate([q_ref[:, (g * Q_PER_KV + r) * HEAD_DIM:(g * Q_PER_KV + r + 1) * HEAD_DIM]
                              for r in range(Q_PER_KV)], axis=0)
        qgs.append(qg)

        kc = kvc_ref[0, :, lo:lo + HEAD_DIM].astype(BF16)
        vc = kvc_ref[1, :, lo:lo + HEAD_DIM].astype(BF16)
        s_c = jnp.where(mask_c, _dot_nt(kc, qg), NEG)
        e_c = jnp.exp(s_c - jnp.max(s_c, axis=0, keepdims=True))
        p_c = jnp.where(mask_c, e_c / jnp.sum(e_c, axis=0, keepdims=True), 0.0)
        o_cs.append(_dot_tn(vc, p_c.astype(BF16)))

        psum = p_c[:, 0:_TQ]
        for r in range(1, Q_PER_KV):
            psum = psum + p_c[:, r * _TQ:(r + 1) * _TQ]
        imp = jnp.dot(mt_ref[...], psum, precision=HIGHEST, preferred_element_type=F32)
        score = jnp.where(valid, imp + jnp.where(forced, FORCE_BONUS, 0.0), -FORCE_BONUS)
        score = jnp.where(blk < n_sel, score, -jnp.inf)
        sel = _rank_select(score, blk, n_sel, axis=0)
        bias_sel = jnp.where(sel > 0.5, 0.0, NEG)
        for c in range(seq // _KC):
            bias_ref[g, c, 0:bpc, :] = bias_sel[c * bpc:(c + 1) * bpc, :]

    def sel_chunk(c, carry, causal):
        k0 = pl.multiple_of(c * _KC, _KC)
        out = []
        for g in range(N_KV):
            lo = g * HEAD_DIM
            m, l, acc = carry[g]
            ks = kh_ref[pl.ds(k0, _KC), lo:lo + HEAD_DIM]
            vst = vt_ref[lo:lo + HEAD_DIM, pl.ds(k0, _KC)]
            bb = bias_ref[g, c, 0:bpc, :]
            bias = jnp.concatenate([jnp.broadcast_to(bb[j:j + 1, :], (SEL_BLOCK, _TQ)) for j in range(bpc)], axis=0)
            s = _dot_nt(ks, qgs[g])
            s = jnp.concatenate([s[:, r * _TQ:(r + 1) * _TQ] + bias for r in range(Q_PER_KV)], axis=1)
            if causal:
                s = jnp.where(k0 + kidx <= pos, s, NEG)
            m_new = jnp.maximum(m, jnp.max(s, axis=0, keepdims=True))
            alpha = jnp.exp(m - m_new)
            p = jnp.exp(s - m_new)
            l = alpha * l + jnp.sum(p, axis=0, keepdims=True)
            acc = alpha * acc + _dot(vst, p.astype(BF16))
            out.append((m_new, l, acc))
        return tuple(out)

    carry = tuple((jnp.full((1, cols), NEG, F32), jnp.zeros((1, cols), F32), jnp.zeros((HEAD_DIM, cols), F32))
                  for _ in range(N_KV))
    carry = lax.fori_loop(0, n_kc - 1, functools.partial(sel_chunk, causal=False), carry)
    carry = sel_chunk(n_kc - 1, carry, causal=True)

    for g in range(N_KV):
        lo = g * HEAD_DIM
        qg, o_c = qgs[g], o_cs[g]
        _, l_s, acc_s = carry[g]
        o_s = acc_s * jnp.where(l_s > 0.0, 1.0 / l_s, 0.0)

        kw = kh_ref[pl.ds(w_start, win_keys), 2 * KV_W + lo:2 * KV_W + lo + HEAD_DIM]
        vwt = vt_ref[KV_W + lo:KV_W + lo + HEAD_DIM, pl.ds(w_start, win_keys)]
        s_w = _dot_nt(kw, qg) + bias_w
        e_w = jnp.exp(s_w - jnp.max(s_w, axis=0, keepdims=True))
        o_w = _dot(vwt, e_w.astype(BF16)) * (1.0 / jnp.sum(e_w, axis=0, keepdims=True))

        for r in range(Q_PER_KV):
            h = g * Q_PER_KV + r
            sl = slice(r * _TQ, (r + 1) * _TQ)
            o = (gnt_ref[3 * h:3 * h + 1, :] * o_c[:, sl] + gnt_ref[3 * h + 1:3 * h + 2, :] * o_s[:, sl]
                 + gnt_ref[3 * h + 2:3 * h + 3, :] * o_w[:, sl])
            o_ref[h * HEAD_DIM:(h + 1) * HEAD_DIM, :] = o.astype(BF16)


def _nsa_prompt(q, gn, kvc, kvh, vt, batch, seq):
    n_chunks = seq // CMP_STRIDE
    n_cmp = (seq - CMP_BLOCK) // CMP_STRIDE + 1
    n_sel = seq // SEL_BLOCK
    n_sel_pad = -(-n_sel // SUBLANES) * SUBLANES
    mt = _cmp_to_sel_t(n_chunks, n_cmp, n_sel, n_sel_pad)
    gnt = gn.reshape(batch, seq, GATE_PAD).transpose(0, 2, 1)
    out = pl.pallas_call(
        functools.partial(_nsa_prompt_kernel, seq=seq, n_cmp=n_cmp, n_sel=n_sel, n_sel_pad=n_sel_pad),
        grid=(batch, seq // _TQ),
        in_specs=[pl.BlockSpec((None, _TQ, D_MODEL), lambda b, i: (b, i, 0)),
                  pl.BlockSpec((None, GATE_PAD, _TQ), lambda b, i: (b, 0, i)),
                  pl.BlockSpec((None, 2, n_chunks, KV_W), lambda b, i: (b, 0, 0, 0)),
                  pl.BlockSpec((None, seq, 4 * KV_W), lambda b, i: (b, 0, 0)),
                  pl.BlockSpec((2 * KV_W, seq), lambda b, i: (0, b)),
                  pl.BlockSpec((n_sel_pad, n_chunks), lambda b, i: (0, 0))],
        out_specs=pl.BlockSpec((None, D_MODEL, _TQ), lambda b, i: (b, 0, i)),
        out_shape=jax.ShapeDtypeStruct((batch, D_MODEL, seq), BF16),
        scratch_shapes=[pltpu.VMEM((N_KV, seq // _KC, SUBLANES, _TQ), F32)],
        compiler_params=_cparams(("parallel", "arbitrary")),
        name="nsa_prompt",
    )(q.reshape(batch, seq, D_MODEL), gnt, kvc, kvh.reshape(batch, seq, 4 * KV_W), vt, mt)
    return out.transpose(0, 2, 1).reshape(batch * seq, D_MODEL)


def _cmp_to_sel_t(n_chunks, n_cmp, n_sel, n_sel_pad):
    cs = np.arange(n_chunks)[None, :] * CMP_STRIDE
    ss = np.arange(n_sel_pad)[:, None] * SEL_BLOCK
    m = (cs < ss + SEL_BLOCK) & (cs + CMP_BLOCK > ss)
    m = m & (np.arange(n_chunks)[None, :] < n_cmp) & (np.arange(n_sel_pad)[:, None] < n_sel)
    return jnp.asarray(m, F32)


def _block_expand(rows, keys):
    return jnp.asarray(np.arange(keys)[None, :] // SEL_BLOCK == np.arange(rows)[:, None], BF16)


_NEW_PAD = 128


def _slc_page_copy(cache_ref, buf_ref, sem_ref, pt_ref, layer, sample, page, slot, n_pages):
    pid = pt_ref[sample * n_pages + page]
    return pltpu.make_async_copy(cache_ref.at[layer, pid, pl.ds(2 * KV_W, 2 * KV_W), :],
                                 buf_ref.at[slot, :, pl.ds(page * PAGE_SIZE, PAGE_SIZE)], sem_ref.at[slot])


def _joint_softmax(s_a, m_a, s_b, m_b):
    mx = jnp.maximum(jnp.max(jnp.where(m_a, s_a, NEG), axis=-1, keepdims=True),
                     jnp.max(jnp.where(m_b, s_b, NEG), axis=-1, keepdims=True))
    e_a = jnp.where(m_a, jnp.exp(s_a - mx), 0.0)
    e_b = jnp.where(m_b, jnp.exp(s_b - mx), 0.0)
    den = jnp.sum(e_a, axis=-1, keepdims=True) + jnp.sum(e_b, axis=-1, keepdims=True)
    return e_a, e_b, jnp.where(den > 0.0, 1.0 / den, 0.0)


def _nsa_sample_kernel(pt_ref, qbd_ref, gate_ref, kvc_ref, new_ref, win_ref, mt_ref, ex_ref, rr_ref, meta_ref,
                       cache_ref, o_ref, buf_ref, sem_ref, *, layer, n_pages, t_new, n_cmp, n_sel, dec_b):
    smp = pl.program_id(0)
    slot = smp % 2
    past = n_pages * PAGE_SIZE
    rows = N_KV * t_new * Q_PER_KV
    win_rows = win_ref.shape[1]

    def issue(s, sl):
        for page in range(n_pages):
            _slc_page_copy(cache_ref, buf_ref, sem_ref, pt_ref, layer, s, page, sl, n_pages).start()

    @pl.when(smp == 0)
    def _():
        issue(smp, slot)

    @pl.when(smp + 1 < dec_b)
    def _():
        issue(smp + 1, 1 - slot)

    for page in range(n_pages):
        _slc_page_copy(cache_ref, buf_ref, sem_ref, pt_ref, layer, 0, page, slot, n_pages).wait()

    qbd = qbd_ref[...]
    tok = meta_ref[:, 0:1]
    grp = meta_ref[:, 1:2]
    pos = past + tok

    n_chunks = kvc_ref.shape[1]
    n_idx = lax.broadcasted_iota(jnp.int32, (1, n_chunks), 1)
    mask_c = (n_idx * CMP_STRIDE + (CMP_BLOCK - 1) <= pos) & (n_idx < n_cmp)
    p_c = _masked_softmax_rows(_dot_nt(qbd, kvc_ref[0].astype(BF16)), mask_c)
    o_c = _dot(p_c.astype(BF16), kvc_ref[1].astype(BF16))

    imp = jnp.dot(rr_ref[...], jnp.dot(p_c, mt_ref[...], precision=HIGHEST, preferred_element_type=F32),
                  precision=HIGHEST, preferred_element_type=F32)
    blk = lax.broadcasted_iota(jnp.int32, (rows, LANES), 1)
    cur = jnp.right_shift(pos, SEL_SHIFT)
    valid = (blk <= cur) & (blk < n_sel)
    forced = (blk == 0) | (blk == cur) | (blk == cur - 1)
    score = jnp.where(valid, imp + jnp.where(forced, FORCE_BONUS, 0.0), -FORCE_BONUS)
    score = jnp.where(blk < n_sel, score, -jnp.inf)
    sel = _rank_select(score, blk, n_sel, axis=1)
    mask_s = _dot(sel.astype(BF16), ex_ref[...]) > 0.5

    i_new = lax.broadcasted_iota(jnp.int32, (1, _NEW_PAD), 1)
    new_ok = (i_new < t_new) & (i_new <= tok)
    new_blk = past // SEL_BLOCK
    mask_sn = new_ok & (sel[:, new_blk:new_blk + 1] > 0.5)

    ks = buf_ref[slot, 0:KV_W, :].astype(BF16)
    vs = buf_ref[slot, KV_W:2 * KV_W, :].astype(BF16)
    s_s = _dot(qbd, ks)
    s_sn = _dot_nt(qbd, new_ref[:, 0:KV_W])
    e_s, e_sn, inv_s = _joint_softmax(s_s, mask_s, s_sn, mask_sn)
    o_s = (_dot_nt(e_s.astype(BF16), vs) + _dot(e_sn.astype(BF16), new_ref[:, KV_W:2 * KV_W])) * inv_s

    kw = win_ref[0:KV_W, :].astype(BF16)
    vw = win_ref[KV_W:2 * KV_W, :].astype(BF16)
    i_w = lax.broadcasted_iota(jnp.int32, (1, win_rows), 1)
    mask_w = i_w > tok + (win_rows - WINDOW)
    s_w = _dot(qbd, kw)
    s_wn = _dot_nt(qbd, new_ref[:, 2 * KV_W:3 * KV_W])
    e_w, e_wn, inv_w = _joint_softmax(s_w, mask_w, s_wn, new_ok)
    o_w = (_dot_nt(e_w.astype(BF16), vw) + _dot(e_wn.astype(BF16), new_ref[:, 3 * KV_W:4 * KV_W])) * inv_w

    o_all = gate_ref[:, 0:1] * o_c + gate_ref[:, 1:2] * o_s + gate_ref[:, 2:3] * o_w
    out = jnp.zeros((rows, HEAD_DIM), F32)
    for g in range(N_KV):
        out = out + jnp.where(grp == g, o_all[:, g * HEAD_DIM:(g + 1) * HEAD_DIM], 0.0)
    o_ref[...] = out


def _nsa_sample(page_table, qbd, gates, kvc, new_rows, win_t, cache_t, layer, t_new):
    dec_b, n_pages = page_table.shape
    past = n_pages * PAGE_SIZE
    rows = N_KV * t_new * Q_PER_KV
    n_chunks = kvc.shape[3]
    n_cmp = (past + t_new - CMP_BLOCK) // CMP_STRIDE + 1
    n_sel = -(-(past + t_new) // SEL_BLOCK)
    win_rows = win_t.shape[-1]
    mt = _cmp_to_sel_t(n_chunks, n_cmp, n_sel, LANES).T
    ex = _block_expand(LANES, past)
    rr = jnp.asarray(np.arange(rows)[:, None] // Q_PER_KV == np.arange(rows)[None, :] // Q_PER_KV, F32)
    meta_np = np.zeros((rows, LANES), np.int32)
    meta_np[:, 0] = (np.arange(rows) // Q_PER_KV) % t_new
    meta_np[:, 1] = np.arange(rows) // (t_new * Q_PER_KV)
    meta = jnp.asarray(meta_np)
    per = lambda *shape: pl.BlockSpec((None,) + shape, lambda s, pt: (s,) + (0,) * len(shape))
    const = lambda *shape: pl.BlockSpec(shape, lambda s, pt: (0,) * len(shape))
    lper = lambda *shape: pl.BlockSpec((None, None) + shape, lambda s, pt: (layer, s) + (0,) * len(shape))
    grid_spec = pltpu.PrefetchScalarGridSpec(
        num_scalar_prefetch=1,
        grid=(dec_b,),
        in_specs=[per(rows, KV_W), per(rows, LANES), lper(2, n_chunks, KV_W), per(_NEW_PAD, 4 * KV_W),
                  lper(2 * KV_W, win_rows), const(n_chunks, LANES), const(LANES, past), const(rows, rows),
                  const(rows, LANES), pl.BlockSpec(memory_space=pl.ANY)],
        out_specs=per(rows, HEAD_DIM),
        scratch_shapes=[pltpu.VMEM((2, 2 * KV_W, past), F32), pltpu.SemaphoreType.DMA((2,))])
    return pl.pallas_call(
        functools.partial(_nsa_sample_kernel, layer=layer, n_pages=n_pages, t_new=t_new, n_cmp=n_cmp, n_sel=n_sel,
                          dec_b=dec_b),
        grid_spec=grid_spec,
        out_shape=jax.ShapeDtypeStruct((dec_b, rows, HEAD_DIM), F32),
        compiler_params=_cparams(("arbitrary",)),
        name="nsa_sample",
    )(page_table.reshape(-1), qbd, gates, kvc, new_rows, win_t, mt, ex, rr, meta, cache_t)


def _merge_kernel(x_ref, ca_ref, on_ref, gc_ref, gm_ref, wc_ref, wn_ref, wg_ref, wo_ref, g_ref, b_ref, h_ref, *, alpha):
    ya = _dot(ca_ref[...], wc_ref[...])
    yb = _dot(on_ref[...], wn_ref[...])
    yc = _dot(gc_ref[...], wg_ref[...])
    m = (gm_ref[:, 0:D_MODEL] * ya + gm_ref[:, D_MODEL:2 * D_MODEL] * yb + gm_ref[:, 2 * D_MODEL:3 * D_MODEL] * yc)
    mix = _dot(m.astype(BF16), wo_ref[...])
    h_ref[...] = _layer_norm(alpha * x_ref[...] + mix, g_ref[...], b_ref[...])


def _merge(x, ca, on, gc, gm, wc, wn, wg, wo, ln_g, ln_b, alpha, layer):
    n = x.shape[0]
    tm = _pick(n, (512, 256, 128, 64, 32, 16, 8))
    row = lambda w: pl.BlockSpec((tm, w), lambda i: (i, 0))
    wspec = pl.BlockSpec((None, D_MODEL, D_MODEL), lambda i: (layer, 0, 0), pipeline_mode=pl.Buffered(1))
    vec = pl.BlockSpec((1, D_MODEL), lambda i: (0, 0))
    return pl.pallas_call(
        functools.partial(_merge_kernel, alpha=alpha),
        grid=(n // tm,),
        in_specs=[row(D_MODEL), row(D_MODEL), row(D_MODEL), row(D_MODEL), row(3 * D_MODEL),
                  wspec, wspec, wspec, wspec, vec, vec],
        out_specs=row(D_MODEL),
        out_shape=jax.ShapeDtypeStruct((n, D_MODEL), F32),
        compiler_params=_cparams(("parallel",)),
        name="merge",
    )(x, ca, on, gc, gm, wc, wn, wg, wo, ln_g[None], ln_b[None])


_ROUTER_PAD = 128


_MOE_SUB = 256


def _split3_bf16(x):
    hi = x.astype(BF16)
    r1 = x - hi.astype(F32)
    mid = r1.astype(BF16)
    lo = (r1 - mid.astype(F32)).astype(BF16)
    return hi, mid, lo


def _moe_route_kernel(h_ref, wr_ref, br_ref, ltri_ref, ustrict_ref, xs_ref, wts_ref, pt_ref, flag_ref):
    tm = h_ref.shape[0]
    lane = lax.broadcasted_iota(jnp.int32, (1, _ROUTER_PAD), 1)
    h = h_ref[...]
    logit = jnp.dot(h, wr_ref[...], precision=HIGHEST, preferred_element_type=F32) + br_ref[...]
    gl = jnp.where(lane < N_GROUPS, logit, -jnp.inf)
    gmax = jnp.max(gl, axis=-1, keepdims=True)
    g_sel = jnp.min(jnp.where(gl == gmax, lane, _ROUTER_PAD), axis=-1, keepdims=True)
    p_grp = 1.0 / jnp.sum(jnp.exp(gl - gmax), axis=-1, keepdims=True)
    lane_grp = jnp.right_shift(lane - N_GROUPS, EPG_SHIFT)
    in_grp = (lane >= N_GROUPS) & (lane < N_GROUPS + N_EXPERTS) & (lane_grp == g_sel)
    el = jnp.where(in_grp, logit, -jnp.inf)
    v1 = jnp.max(el, axis=-1, keepdims=True)
    i1 = jnp.min(jnp.where(el == v1, lane, _ROUTER_PAD), axis=-1, keepdims=True)
    el2 = jnp.where(lane == i1, -jnp.inf, el)
    v2 = jnp.max(el2, axis=-1, keepdims=True)
    i2 = jnp.min(jnp.where(el2 == v2, lane, _ROUTER_PAD), axis=-1, keepdims=True)
    e2 = jnp.exp(v2 - v1)
    wt = jnp.where(lane == i1, 1.0 / (1.0 + e2), jnp.where(lane == i2, e2 / (1.0 + e2), 0.0)) * p_grp

    onehot = jnp.where(lane == g_sel, 1.0, 0.0)
    cum = _dot(ltri_ref[...], onehot.astype(BF16))
    cnt = jnp.broadcast_to(cum[tm - 1:tm, :], (SUBLANES, _ROUTER_PAD))
    off = jnp.dot(cnt, ustrict_ref[...], precision=HIGHEST, preferred_element_type=F32)[0:1, :]
    dst = jnp.sum(onehot * (off + cum - 1.0), axis=-1, keepdims=True)
    dst_row = jnp.broadcast_to(dst, (tm, LANES)).T[0:1, :]
    s_col = lax.broadcasted_iota(jnp.int32, (tm, 1), 0).astype(F32)
    s_row = lax.broadcasted_iota(jnp.int32, (1, tm), 1).astype(F32)
    p_mat = jnp.where(s_col == dst_row, 1.0, 0.0).astype(BF16)
    pt_ref[...] = jnp.where(s_row == dst, 1.0, 0.0).astype(BF16)

    xs_ref[...] = _dot(p_mat, h.astype(BF16)).astype(BF16)
    hi, mid, lo = _split3_bf16(wt)
    wts = _dot(p_mat, hi) + _dot(p_mat, mid) + _dot(p_mat, lo)
    wts_ref[...] = wts
    rows = []
    for j in range(tm // _MOE_SUB):
        rows.append(jnp.max(wts[j * _MOE_SUB:(j + 1) * _MOE_SUB], axis=0, keepdims=True))
    rows.append(jnp.zeros((SUBLANES - tm // _MOE_SUB, _ROUTER_PAD), F32))
    flag_ref[...] = jnp.where(jnp.concatenate(rows, axis=0) > 0.0, 1, 0).astype(jnp.int32)


def _moe_expert_kernel(flag_ref, h_ref, xs_ref, wts_ref, pt_ref, wg_ref, wu_ref, wd_ref, g_ref, b_ref, o_ref, acc_ref,
                       *, alpha):
    i = pl.program_id(0)
    e = pl.program_id(1)
    tm = h_ref.shape[0]
    lane = lax.broadcasted_iota(jnp.int32, (1, _ROUTER_PAD), 1)

    @pl.when(e == 0)
    def _():
        acc_ref[...] = jnp.zeros_like(acc_ref)

    for j in range(tm // _MOE_SUB):
        @pl.when(flag_ref[(i * SUBLANES + j) * N_EXPERTS + e] != 0)
        def _():
            rows = pl.ds(j * _MOE_SUB, _MOE_SUB)
            x = xs_ref[rows, :]
            wcol = jnp.sum(jnp.where(lane == N_GROUPS + e, wts_ref[rows, :], 0.0), axis=-1, keepdims=True)
            a = _dot(x, wg_ref[...])
            b = _dot(x, wu_ref[...])
            hid = (a * _sigmoid(a)) * b * wcol
            acc_ref[rows, :] += _dot(hid.astype(BF16), wd_ref[...])

    @pl.when(e == N_EXPERTS - 1)
    def _():
        acc = acc_ref[...]
        hi = acc.astype(BF16)
        lo = (acc - hi.astype(F32)).astype(BF16)
        f = _dot(pt_ref[...], hi) + _dot(pt_ref[...], lo)
        o_ref[...] = _layer_norm(alpha * h_ref[...] + f, g_ref[...], b_ref[...])


def _moe(h, wr, br, wg, wu, wd, ln_g, ln_b, alpha, layer):
    n = h.shape[0]
    tm = _pick(n, (1024, 512, 256))
    n_tiles = n // tm
    ltri = jnp.asarray(np.tril(np.ones((tm, tm), np.float32)), BF16)
    ustrict = jnp.asarray(np.triu(np.ones((_ROUTER_PAD, _ROUTER_PAD), np.float32), 1), F32)
    row1 = lambda w: pl.BlockSpec((tm, w), lambda i: (i, 0))
    xs, wts, pt, flags = pl.pallas_call(
        _moe_route_kernel,
        grid=(n_tiles,),
        in_specs=[row1(D_MODEL),
                  pl.BlockSpec((None, D_MODEL, _ROUTER_PAD), lambda i: (layer, 0, 0)),
                  pl.BlockSpec((None, 1, _ROUTER_PAD), lambda i: (layer, 0, 0)),
                  pl.BlockSpec((tm, tm), lambda i: (0, 0)),
                  pl.BlockSpec((_ROUTER_PAD, _ROUTER_PAD), lambda i: (0, 0))],
        out_specs=[row1(D_MODEL), row1(_ROUTER_PAD), pl.BlockSpec((None, tm, tm), lambda i: (i, 0, 0)),
                   pl.BlockSpec((None, SUBLANES, _ROUTER_PAD), lambda i: (i, 0, 0))],
        out_shape=[jax.ShapeDtypeStruct((n, D_MODEL), BF16), jax.ShapeDtypeStruct((n, _ROUTER_PAD), F32),
                   jax.ShapeDtypeStruct((n_tiles, tm, tm), BF16),
                   jax.ShapeDtypeStruct((n_tiles, SUBLANES, _ROUTER_PAD), jnp.int32)],
        compiler_params=_cparams(("parallel",)),
        name="moe_route",
    )(h, wr, br, ltri, ustrict)
    flags = flags[:, :, N_GROUPS:N_GROUPS + N_EXPERTS].reshape(-1)
    row = lambda w: pl.BlockSpec((tm, w), lambda i, e, fl: (i, 0))
    vec = pl.BlockSpec((1, D_MODEL), lambda i, e, fl: (0, 0))
    grid_spec = pltpu.PrefetchScalarGridSpec(
        num_scalar_prefetch=1,
        grid=(n_tiles, N_EXPERTS),
        in_specs=[row(D_MODEL), row(D_MODEL), row(_ROUTER_PAD),
                  pl.BlockSpec((None, tm, tm), lambda i, e, fl: (i, 0, 0)),
                  pl.BlockSpec((None, None, D_MODEL, EXPERT_HIDDEN), lambda i, e, fl: (layer, e, 0, 0)),
                  pl.BlockSpec((None, None, D_MODEL, EXPERT_HIDDEN), lambda i, e, fl: (layer, e, 0, 0)),
                  pl.BlockSpec((None, None, EXPERT_HIDDEN, D_MODEL), lambda i, e, fl: (layer, e, 0, 0)),
                  vec, vec],
        out_specs=row(D_MODEL),
        scratch_shapes=[pltpu.VMEM((tm, D_MODEL), F32)])
    return pl.pallas_call(
        functools.partial(_moe_expert_kernel, alpha=alpha),
        grid_spec=grid_spec,
        out_shape=jax.ShapeDtypeStruct((n, D_MODEL), F32),
        compiler_params=_cparams(("parallel", "arbitrary")),
        name="moe_experts",
    )(flags, h, xs, wts, pt, wg, wu, wd, ln_g[None], ln_b[None])


def _win_update_kernel(w_ref, n_ref, o_ref, *, t_new):
    width = w_ref.shape[1]
    x = pltpu.roll(w_ref[...], shift=width - t_new, axis=1)
    lane = lax.broadcasted_iota(jnp.int32, (1, LANES), 1)
    o_ref[:, 0:width - LANES] = x[:, 0:width - LANES]
    o_ref[:, width - LANES:] = jnp.where(lane >= LANES - t_new, n_ref[...], x[:, width - LANES:])


def _win_update(win_t, new_t, t_new):
    depth, dec_b, feat, width = win_t.shape
    rows = depth * dec_b * feat
    tr = _pick(rows, (4096, 2048, 1024, 512))
    out = pl.pallas_call(
        functools.partial(_win_update_kernel, t_new=t_new),
        grid=(rows // tr,),
        in_specs=[pl.BlockSpec((tr, width), lambda i: (i, 0)), pl.BlockSpec((tr, LANES), lambda i: (i, 0))],
        out_specs=pl.BlockSpec((tr, width), lambda i: (i, 0)),
        out_shape=jax.ShapeDtypeStruct((rows, width), win_t.dtype),
        compiler_params=_cparams(("parallel",)),
        name="win_update",
    )(win_t.reshape(rows, width), new_t.reshape(rows, LANES))
    return out.reshape(depth, dec_b, feat, width)


def _prep_in_proj(w_in, b_in):
    c_gn = 2 * D_MODEL + D_MODEL + KV_COLS
    n_gn = 3 * N_HEADS
    pad = GATE_PAD - n_gn

    def rearr(a):
        z = jnp.zeros(a.shape[:-1] + (pad,), a.dtype)
        return jnp.concatenate([a[..., :c_gn], a[..., c_gn + n_gn:], a[..., c_gn:c_gn + n_gn], z], axis=-1)

    return rearr(w_in).astype(BF16), rearr(b_in)[:, None, :]


def _prep_cmp(cmp_pe, cmp_w):
    depth = cmp_w.shape[0]
    w = cmp_w.reshape(depth, 2, CMP_SUB, CMP_STRIDE, HEAD_DIM, HEAD_DIM)
    eye = jnp.eye(N_KV, dtype=cmp_w.dtype)
    wbd = jnp.einsum('zshlde,gk->zslgdhke', w, eye).reshape(depth, 2, _CMP_K, _CMP_N).astype(BF16)
    pe = cmp_pe.reshape(depth, 2, CMP_SUB, CMP_STRIDE, 1, HEAD_DIM)
    pe = jnp.broadcast_to(pe, (depth, 2, CMP_SUB, CMP_STRIDE, N_KV, HEAD_DIM)).reshape(depth, 2, CMP_SUB, _CMP_K)
    pe8 = jnp.concatenate([pe, jnp.zeros((depth, 2, SUBLANES - CMP_SUB, _CMP_K), pe.dtype)], axis=2)
    return pe8, wbd


def _deinterleave_perm():
    cpp = PAGE_SIZE // CMP_STRIDE
    r = np.arange(PAGE_SIZE)
    src = (r % cpp) * CMP_STRIDE + r // cpp
    return jnp.asarray(src[:, None] == np.arange(PAGE_SIZE)[None, :], BF16)


def kernel(x_prompt, x_sample, cache_kv, state_win, state_conv, page_table, w_in, b_in, conv_w, conv_b, conv_ln_g, conv_ln_b, w_up_conv, cmp_pe, cmp_w, w_up_nsa, gm_ln_g, gm_ln_b, gm_ws, gm_bs, w_up_gm, w_o, ln_g, ln_b, router_g_w, router_g_b, router_e_w, router_e_b, moe_w_gate, moe_w_up, moe_w_down):
    depth = w_in.shape[0]
    batch, seq, _ = x_prompt.shape
    dec_b, t_new, _ = x_sample.shape
    n_pages = page_table.shape[1]
    past = n_pages * PAGE_SIZE
    win_buf = state_win.shape[2]
    alpha = float((2 * depth) ** 0.25)
    assert seq % _KC == 0 and seq % GM_CHUNK == 0 and seq >= WINDOW
    assert t_new < CMP_STRIDE and t_new <= _NEW_PAD and past % SEL_BLOCK == 0 and win_buf == min(WINDOW, past)

    w_in_r, b_in_r = _prep_in_proj(w_in, b_in)
    c_kv = 2 * D_MODEL + D_MODEL
    v_slc, v_win = slice(c_kv + 3 * KV_W, c_kv + 4 * KV_W), slice(c_kv + 5 * KV_W, c_kv + 6 * KV_W)
    wvt = jnp.concatenate([w_in[:, :, v_slc], w_in[:, :, v_win]], axis=2).astype(BF16).transpose(0, 2, 1)
    bvt = jnp.concatenate([b_in[:, v_slc], b_in[:, v_win]], axis=1)
    bvt = jnp.broadcast_to(bvt[:, :, None], (depth, 2 * KV_W, LANES))
    pe8, wbd = _prep_cmp(cmp_pe, cmp_w)
    wc16, wn16, wg16, wo16 = (w.astype(BF16) for w in (w_up_conv, w_up_nsa, w_up_gm, w_o))
    mg16, mu16, md16 = moe_w_gate.astype(BF16), moe_w_up.astype(BF16), moe_w_down.astype(BF16)
    zr = jnp.zeros((depth, D_MODEL, _ROUTER_PAD - N_GROUPS - N_EXPERTS), F32)
    w_router = jnp.concatenate([router_g_w, router_e_w.reshape(depth, D_MODEL, N_EXPERTS), zr], axis=-1)
    zb = jnp.zeros((depth, _ROUTER_PAD - N_GROUPS - N_EXPERTS), F32)
    b_router = jnp.concatenate([router_g_b, router_e_b.reshape(depth, N_EXPERTS), zb], axis=-1)[:, None, :]
    gw = D_MODEL // GM_GROUPS
    wrow = jnp.repeat(gm_ws[:, :, :t_new, :t_new].transpose(0, 2, 3, 1).reshape(depth, t_new * t_new, GM_GROUPS), gw, axis=-1)
    bsrow = jnp.repeat(gm_bs[:, :, :t_new].transpose(0, 2, 1), gw, axis=-1)

    cache_t = cache_kv.transpose(0, 1, 3, 4, 5, 2).reshape(depth, cache_kv.shape[1], N_KV_SLOTS * KV_W, PAGE_SIZE)
    win_t_all = state_win.transpose(0, 1, 3, 4, 5, 2).reshape(depth, dec_b, 2 * KV_W, win_buf)

    kvc_s_all = _cmp_sample(page_table, cache_t, _deinterleave_perm(), pe8, wbd)

    xp = x_prompt.reshape(batch * seq, D_MODEL)
    xs = x_sample.reshape(dec_b * t_new, D_MODEL)
    kvp, kvs, wnp, wns, cvp, cvs, gmv = [], [], [], [], [], [], []
    wb = min(WINDOW, seq)
    eye_g = jnp.eye(N_KV, dtype=BF16)
    for l in range(depth):
        glu, q, kv4, kwn, kvh, u, v, gm, gn, vt = _in_proj(xp, w_in_r, b_in_r, wvt, bvt, l)
        ca = _conv_prompt(glu, conv_w[l], conv_b[l], conv_ln_g[l], conv_ln_b[l], batch, seq)
        gc = _gmlp_prompt(u, v, gm_ln_g[l], gm_ln_b[l], gm_ws[l], gm_bs[l])
        kvc = _cmp_prompt(kv4, pe8, wbd, batch, seq, l)
        on = _nsa_prompt(q, gn, kvc, kvh, vt, batch, seq)
        h = _merge(xp, ca, on, gc, gm, wc16, wn16, wg16, wo16, ln_g[l, 0], ln_b[l, 0], alpha, l)
        xp = _moe(h, w_router, b_router, mg16, mu16, md16, ln_g[l, 1], ln_b[l, 1], alpha, l)
        kvp.append(kv4.reshape(batch, seq, N_KV_SLOTS, N_KV, HEAD_DIM))
        wnp.append(kwn.reshape(batch, seq, 2 * KV_W)[:, seq - wb:].reshape(batch, wb, 2, N_KV, HEAD_DIM))
        cvp.append(glu.reshape(batch, seq, D_MODEL)[:, seq - (CONV_WIDTH - 1):])

        glu, q, kv4, kwn, kvh, u, v, gm, gn, _ = _in_proj(xs, w_in_r, b_in_r, wvt, bvt, l)
        conv_ctx = jnp.concatenate([state_conv[l], glu.reshape(dec_b, t_new, D_MODEL)], axis=1)
        tmaj = lambda a: a.reshape(dec_b, t_new, D_MODEL).transpose(1, 0, 2)
        ca_t, gc_t, vn_t = _sample_seq(conv_ctx.transpose(1, 0, 2), tmaj(u), tmaj(v), conv_w[l], conv_b[l],
                                       conv_ln_g[l], conv_ln_b[l], gm_ln_g[l], gm_ln_b[l], wrow[l], bsrow[l])
        bmaj = lambda a: a.transpose(1, 0, 2).reshape(dec_b * t_new, D_MODEL)
        q5 = q.reshape(dec_b, t_new, N_KV, Q_PER_KV, HEAD_DIM).transpose(0, 2, 1, 3, 4)
        qbd = jnp.einsum('bgtrd,gk->bgtrkd', q5, eye_g).reshape(dec_b, N_KV * t_new * Q_PER_KV, KV_W)
        g5 = gn[:, :3 * N_HEADS].reshape(dec_b, t_new, N_KV, Q_PER_KV, 3).transpose(0, 2, 1, 3, 4)
        gates = jnp.pad(g5.reshape(dec_b, N_KV * t_new * Q_PER_KV, 3), ((0, 0), (0, 0), (0, LANES - 3)))
        new_rows = jnp.pad(kvh.reshape(dec_b, t_new, 4 * KV_W), ((0, 0), (0, _NEW_PAD - t_new), (0, 0)))
        o5 = _nsa_sample(page_table, qbd, gates, kvc_s_all, new_rows, win_t_all, cache_t, l, t_new)
        on = o5.reshape(dec_b, N_KV, t_new, Q_PER_KV, HEAD_DIM).transpose(0, 2, 1, 3, 4)
        on = on.reshape(dec_b * t_new, D_MODEL).astype(BF16)
        h = _merge(xs, bmaj(ca_t), on, bmaj(gc_t), gm, wc16, wn16, wg16, wo16, ln_g[l, 0], ln_b[l, 0], alpha, l)
        xs = _moe(h, w_router, b_router, mg16, mu16, md16, ln_g[l, 1], ln_b[l, 1], alpha, l)
        kvs.append(kv4.reshape(dec_b, t_new, N_KV_SLOTS, N_KV, HEAD_DIM))
        wns.append(kwn.reshape(dec_b, t_new, 2, N_KV, HEAD_DIM))
        cvs.append(conv_ctx[:, t_new:])
        gmv.append(vn_t.transpose(1, 0, 2))

    new_t = jnp.stack(wns).reshape(depth, dec_b, t_new, 2 * KV_W).transpose(0, 1, 3, 2)
    new_t = jnp.pad(new_t, ((0, 0), (0, 0), (0, 0), (LANES - t_new, 0)))
    win_sample = _win_update(win_t_all, new_t, t_new)
    win_sample = win_sample.reshape(depth, dec_b, 2, N_KV, HEAD_DIM, win_buf).transpose(0, 1, 5, 2, 3, 4)
    return (xp.reshape(batch, seq, D_MODEL), xs.reshape(dec_b, t_new, D_MODEL), jnp.stack(kvp), jnp.stack(kvs),
            jnp.stack(wnp), win_sample, jnp.stack(cvp), jnp.stack(cvs), jnp.stack(gmv))
```
